```python
import math
import jax
import jax.numpy as jnp
from jax import lax
import numpy as np

D_MODEL = 2048
BATCH = 16
SEQ = 2048
DEPTH = 4
DEC_BATCH = 1
DEC_SEQ = 16384
PAST_LEN = 128

N_BRANCH = 4
BR = D_MODEL // N_BRANCH
RET_HEADS = 4
RET_DH = BR // RET_HEADS
ROPE_BASE = 10000.0
GLA_HEADS = 4
GLA_DK = BR // (2 * GLA_HEADS)
GLA_DV = BR // GLA_HEADS
GLA_KW = GLA_HEADS * GLA_DK
GLA_RANK = 16
GLA_GATE_NORM = 16.0
GLA_CHUNK = 16
HY_ORDER = 2
HY_EMB = 33
HY_BANDS = (HY_EMB - 1) // 2
HY_HIDDEN = 64
HY_MIN_DECAY = math.log(1e-2) / 1.5
HY_MAX_DECAY = math.log(1e-2) / 0.3
ML_HEADS = 4
ML_DH = BR // ML_HEADS
ML_BLOCK = 4
CHUNK = 128
EPS = 1e-6
NEG = -1e30
RET_COLS = 4 * BR
GLA_COLS = 2 * GLA_KW + 2 * BR + 2 * GLA_RANK
HY_COLS = 4 * BR
ML_COLS = 3 * BR + 4 * ML_HEADS
N_IN = RET_COLS + GLA_COLS + HY_COLS + ML_COLS

kernel_name = 'hybrid_bidir_gated_merge_encoder'


def split_cols(x, sizes):
    return jnp.split(x, [int(s) for s in np.cumsum(sizes)[:-1]], axis=-1)


def flip(x):
    return x[:, ::-1]


def rmsnorm(x, gain):
    xf = x.astype(jnp.float32)
    y = xf * lax.rsqrt(jnp.mean(xf * xf, axis=-1, keepdims=True) + EPS)
    return (y * gain.astype(jnp.float32)).astype(x.dtype)


def head_norm(x, gain, center):
    xf = x.astype(jnp.float32)
    if center:
        xf = xf - jnp.mean(xf, axis=-1, keepdims=True)
    y = xf * lax.rsqrt(jnp.mean(xf * xf, axis=-1, keepdims=True) + EPS)
    if gain is not None:
        y = y * gain.astype(jnp.float32)
    return y


def centred_conv3(x, w, b):
    xp = jnp.pad(x, ((0, 0), (1, 1), (0, 0)))
    return xp[:, :-2] * w[0] + xp[:, 1:-1] * w[1] + xp[:, 2:] * w[2] + b


def rotary(x):
    L, d = x.shape[1], x.shape[-1]
    inv = ROPE_BASE ** (-jnp.arange(0, d, 2, dtype=jnp.float32) / d)
    ang = jnp.arange(L, dtype=jnp.float32)[:, None] * inv[None, :]
    cos = jnp.cos(ang)[None, :, None, :]
    sin = jnp.sin(ang)[None, :, None, :]
    xf = x.astype(jnp.float32)
    x1, x2 = xf[..., : d // 2], xf[..., d // 2:]
    return jnp.concatenate([x1 * cos - x2 * sin, x1 * sin + x2 * cos], axis=-1)


def scan_scalar_gate(q, k, v, log_f, log_i):
    B, L, H, dk = q.shape
    dv = v.shape[-1]
    nc = L // CHUNK
    f32 = jnp.float32
    qc = q.astype(f32).reshape(B, nc, CHUNK, H, dk)
    kc = k.astype(f32).reshape(B, nc, CHUNK, H, dk)
    vc = v.astype(f32).reshape(B, nc, CHUNK, H, dv)
    lf = log_f.astype(f32).reshape(B, nc, CHUNK, H)
    li = log_i.astype(f32).reshape(B, nc, CHUNK, H)
    b = jnp.cumsum(lf, axis=2)
    b_end = b[:, :, -1]
    g = b_end[:, :, None] - b + li
    m_loc = jnp.max(g, axis=2)
    w = jnp.exp(g - m_loc[:, :, None])
    U = jnp.einsum('bnjh,bnjhd,bnjhe->bnhde', w, kc, vc)
    z = jnp.einsum('bnjh,bnjhd->bnhd', w, kc)

    def step(carry, inp):
        S, n, m = carry
        U_c, z_c, m_c, be_c = inp
        m_new = jnp.maximum(be_c + m, m_c)
        a_old = jnp.exp(be_c + m - m_new)
        a_new = jnp.exp(m_c - m_new)
        S_new = a_old[..., None, None] * S + a_new[..., None, None] * U_c
        n_new = a_old[..., None] * n + a_new[..., None] * z_c
        return (S_new, n_new, m_new), (S, n, m)

    init = (jnp.zeros((B, H, dk, dv), f32), jnp.zeros((B, H, dk), f32), jnp.full((B, H), NEG, f32))
    xs = (jnp.moveaxis(U, 1, 0), jnp.moveaxis(z, 1, 0), jnp.moveaxis(m_loc, 1, 0), jnp.moveaxis(b_end, 1, 0))
    _, (S_prev, n_prev, m_prev) = lax.scan(step, init, xs)
    S_prev = jnp.moveaxis(S_prev, 0, 1)
    n_prev = jnp.moveaxis(n_prev, 0, 1)
    m_prev = jnp.moveaxis(m_prev, 0, 1)
    D = b[:, :, :, None, :] - b[:, :, None, :, :] + li[:, :, None, :, :]
    causal = jnp.tril(jnp.ones((CHUNK, CHUNK), dtype=bool))[None, None, :, :, None]
    D = jnp.where(causal, D, NEG)
    m_in = b + m_prev[:, :, None]
    m_t = jnp.maximum(m_in, jnp.max(D, axis=3))
    P = jnp.exp(D - m_t[:, :, :, None]) * jnp.einsum('bnihd,bnjhd->bnijh', qc, kc)
    a_in = jnp.exp(m_in - m_t)
    num = jnp.einsum('bnijh,bnjhe->bnihe', P, vc) + a_in[..., None] * jnp.einsum('bnihd,bnhde->bnihe', qc, S_prev)
    den = jnp.sum(P, axis=3) + a_in * jnp.einsum('bnihd,bnhd->bnih', qc, n_prev)
    return num.reshape(B, L, H, dv), den.reshape(B, L, H), m_t.reshape(B, L, H)


def scan_vector_gate(q, k, v, log_a):
    B, L, H, dk = q.shape
    dv = v.shape[-1]
    C = GLA_CHUNK
    nc = L // C
    f32 = jnp.float32
    qc = q.astype(f32).reshape(B, nc, C, H, dk)
    kc = k.astype(f32).reshape(B, nc, C, H, dk)
    vc = v.astype(f32).reshape(B, nc, C, H, dv)
    b = jnp.cumsum(log_a.astype(f32).reshape(B, nc, C, H, dk), axis=2)
    b_end = b[:, :, -1]
    U = jnp.einsum('bnjhd,bnjhe->bnhde', kc * jnp.exp(b_end[:, :, None] - b), vc)

    def step(S, inp):
        U_c, be_c = inp
        return jnp.exp(be_c)[..., None] * S + U_c, S

    _, S_prev = lax.scan(step, jnp.zeros((B, H, dk, dv), f32), (jnp.moveaxis(U, 1, 0), jnp.moveaxis(b_end, 1, 0)))
    S_prev = jnp.moveaxis(S_prev, 0, 1)
    causal = jnp.tril(jnp.ones((C, C), dtype=bool))[None, None, :, :, None, None]
    diff = jnp.where(causal, b[:, :, :, None] - b[:, :, None, :], NEG)
    A = jnp.einsum('bnihd,bnjhd,bnijhd->bnijh', qc, kc, jnp.exp(diff))
    o = jnp.einsum('bnijh,bnjhe->bnihe', A, vc) + jnp.einsum('bnihd,bnhde->bnihe', qc * jnp.exp(b), S_prev)
    return o.reshape(B, L, H, dv)


def retention_branch(p):
    B, L, _ = p.shape
    q, k, v, g = split_cols(p, (BR, BR, BR, BR))
    q = rotary(q.reshape(B, L, RET_HEADS, RET_DH))
    k = rotary(k.reshape(B, L, RET_HEADS, RET_DH)) * RET_DH ** -0.5
    v = v.reshape(B, L, RET_HEADS, RET_DH).astype(jnp.float32)
    log_gamma = jnp.log(1.0 - 2.0 ** (-5.0 - jnp.arange(RET_HEADS, dtype=jnp.float32)))
    lf = jnp.broadcast_to(log_gamma, (B, L, RET_HEADS))
    li = jnp.zeros((B, L, RET_HEADS), jnp.float32)
    num_f, _, m_f = scan_scalar_gate(q, k, v, lf, li)
    num_b, _, m_b = scan_scalar_gate(flip(q), flip(k), flip(v), flip(lf), li)
    diag = jnp.einsum('blhd,blhd->blh', q, k)[..., None] * v
    ret = num_f * jnp.exp(m_f)[..., None] + flip(num_b * jnp.exp(m_b)[..., None]) - diag
    ret = head_norm(ret, None, center=True).reshape(B, L, BR)
    return (ret * jax.nn.silu(g.astype(jnp.float32))).astype(p.dtype)


def gla_branch(p, w_a, b_a, norm_gain):
    B, L, _ = p.shape
    q, k, v, g, a_f, a_b = split_cols(p, (GLA_KW, GLA_KW, BR, BR, GLA_RANK, GLA_RANK))
    q = q.reshape(B, L, GLA_HEADS, GLA_DK) * GLA_DK ** -0.5
    k = k.reshape(B, L, GLA_HEADS, GLA_DK)
    v = v.reshape(B, L, GLA_HEADS, GLA_DV)

    def log_gate(a, w, bias):
        return (jax.nn.log_sigmoid((a @ w + bias).astype(jnp.float32)) / GLA_GATE_NORM).reshape(B, L, GLA_HEADS, GLA_DK)

    o_f = scan_vector_gate(q, k, v, log_gate(a_f, w_a[0], b_a[0]))
    o_b = flip(scan_vector_gate(flip(q), flip(k), flip(v), flip(log_gate(a_b, w_a[1], b_a[1]))))
    o = head_norm(o_f + o_b, norm_gain, center=False).reshape(B, L, BR)
    return (o * jax.nn.silu(g.astype(jnp.float32))).astype(p.dtype)


def hyena_filters(L, w1, b1, w2, b2, freq, w3):
    pos = jnp.arange(L, dtype=jnp.float32)
    t = pos / (L - 1)
    f = jnp.linspace(1e-4, HY_BANDS - 1, HY_BANDS, dtype=jnp.float32)
    ang = (2.0 * math.pi / L) * pos[:, None] * f[None, :]
    emb = jnp.concatenate([t[:, None], jnp.cos(ang), -jnp.sin(ang)], axis=-1)
    h = jnp.sin(freq * (emb @ w1 + b1))
    h = jnp.sin(freq * (h @ w2 + b2))
    h = (h @ w3).reshape(L, HY_ORDER, 2, BR)
    deltas = jnp.abs(jnp.linspace(HY_MIN_DECAY, HY_MAX_DECAY, BR, dtype=jnp.float32))
    window = jnp.exp(-t[:, None] * deltas[None, :])
    return (h * window[:, None, None, :]).astype(jnp.float32)


def long_conv(z, h_f, h_b, skip):
    L = z.shape[1]
    taps = jnp.concatenate([h_f, jnp.zeros_like(h_f[:1]), h_b[:0:-1]], axis=0)
    taps = taps * lax.rsqrt(jnp.sum(taps * taps, axis=0, keepdims=True) + EPS)
    zf = z.astype(jnp.float32)
    y = jnp.fft.irfft(jnp.fft.rfft(zf, n=2 * L, axis=1) * jnp.fft.rfft(taps, axis=0)[None], n=2 * L, axis=1)[:, :L]
    return y + zf * skip.astype(jnp.float32)


def hyena_branch(p, conv_w, conv_b, w1, b1, w2, b2, freq, w3, skip):
    L = p.shape[1]
    streams, g = p[..., :3 * BR], p[..., 3 * BR:]
    v, x1, x2 = split_cols(centred_conv3(streams, conv_w, conv_b), (BR, BR, BR))
    h = hyena_filters(L, w1, b1, w2, b2, freq, w3)
    z = x1.astype(jnp.float32) * long_conv(v, h[:, 0, 0], h[:, 0, 1], skip[0])
    z = x2.astype(jnp.float32) * long_conv(z, h[:, 1, 0], h[:, 1, 1], skip[1])
    return (z * jax.nn.silu(g.astype(jnp.float32))).astype(p.dtype)


def mlstm_branch(p, conv_w, conv_b, wq, wk, wv, b_i, b_f, norm_gain, skip):
    B, L, _ = p.shape
    f32 = jnp.float32
    u, zg, o_pre, i_f, i_b, f_f, f_b = split_cols(p, (BR, BR, BR) + (ML_HEADS,) * 4)
    xc = jax.nn.silu(centred_conv3(u, conv_w, conv_b))

    def blockdiag(a, w):
        a = a.reshape(B, L, BR // ML_BLOCK, ML_BLOCK)
        return jnp.einsum('blne,nef->blnf', a, w).reshape(B, L, ML_HEADS, ML_DH)

    q = blockdiag(xc, wq)
    k = blockdiag(xc, wk) * ML_DH ** -0.5
    v = blockdiag(u, wv)

    def direction(qd, kd, vd, i_pre, f_pre):
        num, den, m = scan_scalar_gate(qd, kd, vd, jax.nn.log_sigmoid(f_pre.astype(f32)), i_pre.astype(f32))
        return num / jnp.maximum(jnp.abs(den), jnp.exp(-m))[..., None]

    h_f = direction(q, k, v, i_f + b_i[0], f_f + b_f[0])
    h_b = flip(direction(flip(q), flip(k), flip(v), flip(i_b + b_i[1]), flip(f_b + b_f[1])))
    h = jax.nn.sigmoid(o_pre.astype(f32)).reshape(B, L, ML_HEADS, ML_DH) * (h_f + h_b)
    h = head_norm(h, norm_gain.reshape(ML_HEADS, ML_DH), center=True).reshape(B, L, BR)
    h = h + skip.astype(f32) * xc.astype(f32)
    return (h * jax.nn.silu(zg.astype(f32))).astype(p.dtype)


def hybrid_layer(x, c, norm_gain, w_ada, b_ada, w_in, gla_w_a, gla_b_a, gla_norm_gain,
                 hy_conv_w, hy_conv_b, hy_w1, hy_b1, hy_w2, hy_b2, hy_freq, hy_w3, hy_skip,
                 ml_conv_w, ml_conv_b, ml_wq, ml_wk, ml_wv, ml_b_i, ml_b_f, ml_norm_gain, ml_skip,
                 w_branch, w_gate, b_gate, w_out):
    mod = jax.nn.silu(c) @ w_ada + b_ada
    shift, scale, gate = split_cols(mod, (D_MODEL, D_MODEL, D_MODEL))
    h = rmsnorm(x, norm_gain) * (1.0 + scale[:, None, :]) + shift[:, None, :]
    proj = h @ w_in
    p_ret, p_gla, p_hy, p_ml = split_cols(proj, (RET_COLS, GLA_COLS, HY_COLS, ML_COLS))
    branches = (
        retention_branch(p_ret),
        gla_branch(p_gla, gla_w_a, gla_b_a, gla_norm_gain),
        hyena_branch(p_hy, hy_conv_w, hy_conv_b, hy_w1, hy_b1, hy_w2, hy_b2, hy_freq, hy_w3, hy_skip),
        mlstm_branch(p_ml, ml_conv_w, ml_conv_b, ml_wq, ml_wk, ml_wv, ml_b_i, ml_b_f, ml_norm_gain, ml_skip),
    )
    merged = jax.nn.sigmoid(h @ w_gate[0] + b_gate[0]) * (branches[0] @ w_branch[0])
    for bi in range(1, N_BRANCH):
        merged = merged + jax.nn.sigmoid(h @ w_gate[bi] + b_gate[bi]) * (branches[bi] @ w_branch[bi])
    return x + gate[:, None, :] * (merged @ w_out)


def setup_inputs(seed: int = 0) -> dict:
    key = jax.random.key(seed)
    keys = iter(jax.random.split(key, 48))

    def nrm(shape, scale):
        return scale * jax.random.normal(next(keys), shape, jnp.float32)

    D = D_MODEL
    return {
        'x_prompt': nrm((BATCH, SEQ, D), 1.0),
        'x_sample': nrm((DEC_BATCH, DEC_SEQ, D), 1.0),
        'c_prompt': nrm((BATCH, D), 1.0),
        'c_sample': nrm((DEC_BATCH, D), 1.0),
        'norm_gain': 1.0 + nrm((DEPTH, D), 0.1),
        'w_ada': nrm((DEPTH, D, 3 * D), 0.5 * D ** -0.5),
        'b_ada': nrm((DEPTH, 3 * D), 0.02),
        'w_in': nrm((DEPTH, D, N_IN), D ** -0.5),
        'gla_w_a': nrm((DEPTH, 2, GLA_RANK, GLA_KW), GLA_RANK ** -0.5),
        'gla_b_a': nrm((DEPTH, 2, GLA_KW), 0.1),
        'gla_norm_gain': 1.0 + nrm((DEPTH, GLA_DV), 0.1),
        'hy_conv_w': nrm((DEPTH, 3, 3 * BR), 3 ** -0.5),
        'hy_conv_b': nrm((DEPTH, 3 * BR), 0.02),
        'hy_w1': nrm((DEPTH, HY_EMB, HY_HIDDEN), HY_EMB ** -0.5),
        'hy_b1': nrm((DEPTH, HY_HIDDEN), 0.1),
        'hy_w2': nrm((DEPTH, HY_HIDDEN, HY_HIDDEN), HY_HIDDEN ** -0.5),
        'hy_b2': nrm((DEPTH, HY_HIDDEN), 0.1),
        'hy_freq': 1.0 + nrm((DEPTH, HY_HIDDEN), 0.1),
        'hy_w3': nrm((DEPTH, HY_HIDDEN, HY_ORDER * 2 * BR), HY_HIDDEN ** -0.5),
        'hy_skip': nrm((DEPTH, HY_ORDER, BR), 0.5),
        'ml_conv_w': nrm((DEPTH, 3, BR), 3 ** -0.5),
        'ml_conv_b': nrm((DEPTH, BR), 0.02),
        'ml_wq': nrm((DEPTH, BR // ML_BLOCK, ML_BLOCK, ML_BLOCK), 0.5),
        'ml_wk': nrm((DEPTH, BR // ML_BLOCK, ML_BLOCK, ML_BLOCK), 0.5),
        'ml_wv': nrm((DEPTH, BR // ML_BLOCK, ML_BLOCK, ML_BLOCK), 0.5),
        'ml_b_i': nrm((DEPTH, 2, ML_HEADS), 0.1),
        'ml_b_f': jnp.linspace(3.0, 6.0, ML_HEADS, dtype=jnp.float32) + nrm((DEPTH, 2, ML_HEADS), 0.1),
        'ml_norm_gain': 1.0 + nrm((DEPTH, BR), 0.1),
        'ml_skip': 1.0 + nrm((DEPTH, BR), 0.1),
        'w_branch': nrm((DEPTH, N_BRANCH, BR, D), BR ** -0.5),
        'w_gate': nrm((DEPTH, N_BRANCH, D, D), D ** -0.5),
        'b_gate': nrm((DEPTH, N_BRANCH, D), 0.1),
        'w_out': nrm((DEPTH, D, D), D ** -0.5),
        'final_gain': 1.0 + nrm((D,), 0.1),
    }


def reference(x_prompt, x_sample, c_prompt, c_sample, norm_gain, w_ada, b_ada, w_in,
              gla_w_a, gla_b_a, gla_norm_gain,
              hy_conv_w, hy_conv_b, hy_w1, hy_b1, hy_w2, hy_b2, hy_freq, hy_w3, hy_skip,
              ml_conv_w, ml_conv_b, ml_wq, ml_wk, ml_wv, ml_b_i, ml_b_f, ml_norm_gain, ml_skip,
              w_branch, w_gate, b_gate, w_out, final_gain):
    xp = x_prompt
    xs = x_sample
    for l in range(DEPTH):
        lp = (norm_gain[l], w_ada[l], b_ada[l], w_in[l], gla_w_a[l], gla_b_a[l], gla_norm_gain[l],
              hy_conv_w[l], hy_conv_b[l], hy_w1[l], hy_b1[l], hy_w2[l], hy_b2[l], hy_freq[l], hy_w3[l], hy_skip[l],
              ml_conv_w[l], ml_conv_b[l], ml_wq[l], ml_wk[l], ml_wv[l], ml_b_i[l], ml_b_f[l], ml_norm_gain[l], ml_skip[l],
              w_branch[l], w_gate[l], b_gate[l], w_out[l])
        xp = hybrid_layer(xp, c_prompt, *lp)
        xs = hybrid_layer(xs, c_sample, *lp)
    y_prompt = rmsnorm(xp, final_gain)
    y_sample = rmsnorm(xs, final_gain)
    return (y_prompt, y_sample)
```

```python
import functools
import math

import jax
import jax.numpy as jnp
from jax import lax
from jax.experimental import pallas as pl
from jax.experimental.pallas import tpu as pltpu

f32 = jnp.float32
bf16 = jnp.bfloat16

D_MODEL = 2048
DEPTH = 4
BR = 512
HEADS = 4
DH = 128
ROPE_BASE = 10000.0
GLA_DK = 64
GLA_KW = 256
GLA_RANK = 16
GLA_GATE_NORM = 16.0
HY_EMB = 33
HY_EMB_PAD = 40
HY_BANDS = 16
HY_HIDDEN = 64
HY_MIN_DECAY = math.log(1e-2) / 1.5
HY_MAX_DECAY = math.log(1e-2) / 0.3
ML_BLOCK = 4
EPS = 1e-6
NEG = -1e30

BLK_RET = 0
BLK_GLA_QK = 4
BLK_GLA_V = 5
BLK_GLA_G = 6
BLK_HY = 7
BLK_ML = 11
SMALL_COL = 7168
SMALL_BLK = SMALL_COL // 128
N_PROJ = SMALL_COL + 128
LANE_GLA_A = 0
LANE_ML_I = 32
LANE_ML_F = 40

CHUNK = 128
FFT_N2 = 128
FFT_LANES = 4096
FFT_K1T = 8
TM_DENSE = 512
TN_IN = 2432
TN_MERGE = 512
VMEM_LIMIT = 56 * 1024 * 1024

LOG_GAMMA = tuple(math.log(1.0 - 2.0 ** (-5.0 - h)) for h in range(HEADS))


def _params(sem):
    return pltpu.CompilerParams(dimension_semantics=sem, vmem_limit_bytes=VMEM_LIMIT)


def _bdot(a, b):
    return jnp.dot(a.astype(bf16), b.astype(bf16), preferred_element_type=f32)


def _bdot_nt(a, b):
    return lax.dot_general(a.astype(bf16), b.astype(bf16), (((1,), (1,)), ((), ())),
                           preferred_element_type=f32)


def _bdot_tn(a, b):
    return jnp.dot(a.T.astype(bf16), b.astype(bf16), preferred_element_type=f32)


def _dot3(a, b):
    a1 = a.astype(bf16)
    ra = a - a1.astype(f32)
    a2 = ra.astype(bf16)
    a3 = (ra - a2.astype(f32)).astype(bf16)
    b1 = b.astype(bf16)
    rb = b - b1.astype(f32)
    b2 = rb.astype(bf16)
    b3 = (rb - b2.astype(f32)).astype(bf16)
    d = functools.partial(jnp.dot, preferred_element_type=f32)
    return (d(a1, b1) + (d(a1, b2) + d(a2, b1))) + ((d(a2, b2) + d(a1, b3)) + d(a3, b1))


def _cumsum01(tri, x):
    x1 = x.astype(bf16)
    r = x - x1.astype(f32)
    x2 = r.astype(bf16)
    x3 = (r - x2.astype(f32)).astype(bf16)
    d = functools.partial(jnp.dot, preferred_element_type=f32)
    return d(tri, x1) + (d(tri, x2) + d(tri, x3))


def _sigmoid(x):
    return jax.nn.sigmoid(x)


def _silu(x):
    return x * jax.nn.sigmoid(x)


def _log_sigmoid(x):
    return jnp.minimum(x, 0.0) - jnp.log1p(jnp.exp(-jnp.abs(x)))


def _tri(n, rev):
    ii = lax.broadcasted_iota(jnp.int32, (n, n), 0)
    jj = lax.broadcasted_iota(jnp.int32, (n, n), 1)
    return (jj >= ii) if rev else (jj <= ii)


def _prev_rows(x, first):
    row = lax.broadcasted_iota(jnp.int32, x.shape, 0)
    return jnp.where(row == 0, first, pltpu.roll(x, 1, 0))


def _next_rows(x, last):
    n = x.shape[0]
    row = lax.broadcasted_iota(jnp.int32, x.shape, 0)
    return jnp.where(row == n - 1, last, pltpu.roll(x, n - 1, 0))


def _conv3(x, prev_ref, next_ref, has_prev, has_next, w, b):
    first = jnp.where(has_prev, prev_ref[7:8, :], 0.0)
    last = jnp.where(has_next, next_ref[0:1, :], 0.0)
    return _prev_rows(x, first) * w[0:1] + x * w[1:2] + _next_rows(x, last) * w[2:3] + b


def _head_norm(x, center):
    if center:
        x = x - jnp.mean(x, axis=-1, keepdims=True)
    return x * lax.rsqrt(jnp.mean(x * x, axis=-1, keepdims=True) + EPS)


def _ada_kernel(c_ref, w_ref, b_ref, o_ref):
    o_ref[0] = _bdot(_silu(c_ref[...]), w_ref[0]) + b_ref[0]


def _ada_all(c_all, w_ada, b_ada):
    rows = c_all.shape[0]
    tn = 1536
    return pl.pallas_call(
        _ada_kernel,
        grid=(DEPTH, 3 * D_MODEL // tn),
        in_specs=[
            pl.BlockSpec((rows, D_MODEL), lambda l, j: (0, 0)),
            pl.BlockSpec((1, D_MODEL, tn), lambda l, j: (l, 0, j)),
            pl.BlockSpec((1, 1, tn), lambda l, j: (l, 0, j)),
        ],
        out_specs=pl.BlockSpec((1, rows, tn), lambda l, j: (l, 0, j)),
        out_shape=jax.ShapeDtypeStruct((DEPTH, rows, 3 * D_MODEL), f32),
        compiler_params=_params(("arbitrary", "arbitrary")),
        name="ada",
    )(c_all, w_ada, b_ada.reshape(DEPTH, 1, 3 * D_MODEL))


def _modulated_norm(x, mod, gain):
    y = x * lax.rsqrt(jnp.mean(x * x, axis=-1, keepdims=True) + EPS) * gain
    return y * (1.0 + mod[1:2]) + mod[0:1]


def _inproj_kernel(x_ref, mod_ref, gain_ref, w_ref, o_ref, h_ref):
    @pl.when(pl.program_id(1) == 0)
    def _():
        h_ref[...] = _modulated_norm(x_ref[...], mod_ref[0], gain_ref[...]).astype(bf16)

    o_ref[...] = jnp.dot(h_ref[...], w_ref[...], preferred_element_type=f32)


def _inproj(x, mod, gain, w, L):
    T = x.shape[0]
    tm = min(TM_DENSE, L)
    return pl.pallas_call(
        _inproj_kernel,
        grid=(T // tm, N_PROJ // TN_IN),
        in_specs=[
            pl.BlockSpec((tm, D_MODEL), lambda i, j: (i, 0)),
            pl.BlockSpec((1, 3, D_MODEL), lambda i, j: ((i * tm) // L, 0, 0)),
            pl.BlockSpec((1, D_MODEL), lambda i, j: (0, 0)),
            pl.BlockSpec((D_MODEL, TN_IN), lambda i, j: (0, j)),
        ],
        out_specs=pl.BlockSpec((tm, TN_IN), lambda i, j: (i, j)),
        out_shape=jax.ShapeDtypeStruct((T, N_PROJ), f32),
        scratch_shapes=[pltpu.VMEM((tm, D_MODEL), bf16)],
        compiler_params=_params(("arbitrary", "arbitrary")),
        name="inproj",
    )(x, mod, gain, w)


def _merge_kernel(final, x_ref, mod_ref, gain_ref, b0_ref, b1_ref, b2_ref, b3_ref,
                  wg_ref, bg_ref, wb_ref, wo_ref, fg_ref, o_ref, h_ref):
    j = pl.program_id(1)

    @pl.when(j == 0)
    def _():
        h_ref[...] = _modulated_norm(x_ref[...], mod_ref[0], gain_ref[...]).astype(bf16)
        o_ref[...] = jnp.zeros_like(o_ref)

    h = h_ref[...]
    merged = None
    for bi, br_ref in enumerate((b0_ref, b1_ref, b2_ref, b3_ref)):
        g = _sigmoid(jnp.dot(h, wg_ref[bi], preferred_element_type=f32) + bg_ref[bi])
        t = g * jnp.dot(br_ref[...], wb_ref[bi], preferred_element_type=f32)
        merged = t if merged is None else merged + t
    o_ref[...] += jnp.dot(merged.astype(bf16), wo_ref[...], preferred_element_type=f32)

    @pl.when(j == pl.num_programs(1) - 1)
    def _():
        y = x_ref[...] + mod_ref[0][2:3] * o_ref[...]
        if final:
            y = y * lax.rsqrt(jnp.mean(y * y, axis=-1, keepdims=True) + EPS) * fg_ref[...]
        o_ref[...] = y


def _merge(x, mod, gain, branches, wg, bg, wb, wo, final_gain, L, final):
    T = x.shape[0]
    tm = min(TM_DENSE, L)
    tn = TN_MERGE
    br_spec = pl.BlockSpec((tm, BR), lambda i, j: (i, 0))
    return pl.pallas_call(
        functools.partial(_merge_kernel, final),
        grid=(T // tm, D_MODEL // tn),
        in_specs=[
            pl.BlockSpec((tm, D_MODEL), lambda i, j: (i, 0)),
            pl.BlockSpec((1, 3, D_MODEL), lambda i, j: ((i * tm) // L, 0, 0)),
            pl.BlockSpec((1, D_MODEL), lambda i, j: (0, 0)),
            br_spec, br_spec, br_spec, br_spec,
            pl.BlockSpec((4, D_MODEL, tn), lambda i, j: (0, 0, j)),
            pl.BlockSpec((4, 1, tn), lambda i, j: (0, 0, j)),
            pl.BlockSpec((4, BR, tn), lambda i, j: (0, 0, j)),
            pl.BlockSpec((tn, D_MODEL), lambda i, j: (j, 0)),
            pl.BlockSpec((1, D_MODEL), lambda i, j: (0, 0)),
        ],
        out_specs=pl.BlockSpec((tm, D_MODEL), lambda i, j: (i, 0)),
        out_shape=jax.ShapeDtypeStruct((T, D_MODEL), f32),
        scratch_shapes=[pltpu.VMEM((tm, D_MODEL), bf16)],
        compiler_params=_params(("arbitrary", "arbitrary")),
        name="merge",
    )(x, mod, gain, *branches, wg, bg, wb, wo, final_gain)


def _chunk_row(nc, rev):
    if rev:
        return lambda b, c: b * nc + (nc - 1 - c)
    return lambda b, c: b * nc + c


def _tok_spec(nc, rev, blk, width=BR):
    row = _chunk_row(nc, rev)
    return pl.BlockSpec((CHUNK, width), lambda b, c: (row(b, c), blk))


def _pos_spec(nc, rev):
    if rev:
        return pl.BlockSpec((CHUNK, DH), lambda b, c: (nc - 1 - c, 0))
    return pl.BlockSpec((CHUNK, DH), lambda b, c: (c, 0))


def _const_spec(shape):
    nd = len(shape)
    return pl.BlockSpec(shape, lambda b, c: (0,) * nd)


def _ret_kernel(rev, final, *refs):
    if final:
        q_ref, k_ref, v_ref, cos_ref, sin_ref, g_ref, rb_ref, o_ref, s_ref = refs
    else:
        q_ref, k_ref, v_ref, cos_ref, sin_ref, o_ref, s_ref = refs
    C = CHUNK

    @pl.when(pl.program_id(1) == 0)
    def _():
        s_ref[...] = jnp.zeros_like(s_ref)

    cos = cos_ref[...]
    sin = sin_ref[...]
    ii = lax.broadcasted_iota(jnp.int32, (C, C), 0)
    jj = lax.broadcasted_iota(jnp.int32, (C, C), 1)
    col = lax.broadcasted_iota(jnp.int32, (C, 1), 0).astype(f32)
    dist = (ii - jj).astype(f32)
    for h in range(HEADS):
        lg = LOG_GAMMA[h]
        sl = slice(h * DH, (h + 1) * DH)
        q = q_ref[:, sl]
        k = k_ref[:, sl]
        v = v_ref[:, sl]
        q = q * cos + pltpu.roll(q, DH // 2, 1) * sin
        k = (k * cos + pltpu.roll(k, DH // 2, 1) * sin) * (DH ** -0.5)
        if rev:
            gam = jnp.where(jj > ii, jnp.exp(-dist * lg), 0.0)
            qd = jnp.exp((C - col) * lg)
            kd = jnp.exp(col * lg)
        else:
            gam = jnp.where(jj <= ii, jnp.exp(dist * lg), 0.0)
            qd = jnp.exp((col + 1.0) * lg)
            kd = jnp.exp((C - 1.0 - col) * lg)
        s_prev = s_ref[h]
        o = _bdot(_bdot_nt(q, k) * gam, v) + qd * _bdot(q, s_prev)
        s_ref[h] = math.exp(C * lg) * s_prev + _bdot_tn(k * kd, v)
        if final:
            r = _head_norm(o + rb_ref[:, sl], center=True)
            o_ref[:, sl] = (r * _silu(g_ref[:, sl])).astype(o_ref.dtype)
        else:
            o_ref[:, sl] = o


def _ret_branch(proj, cosf, sins, B, L):
    T = B * L
    nc = L // CHUNK

    def call(rev, final, extra):
        in_specs = [_tok_spec(nc, rev, BLK_RET + i) for i in range(3)]
        in_specs += [_pos_spec(nc, rev), _pos_spec(nc, rev)]
        args = [proj, proj, proj, cosf, sins]
        if final:
            in_specs += [_tok_spec(nc, rev, BLK_RET + 3), _tok_spec(nc, rev, 0)]
            args += [proj, extra]
        return pl.pallas_call(
            functools.partial(_ret_kernel, rev, final),
            grid=(B, nc),
            in_specs=in_specs,
            out_specs=_tok_spec(nc, rev, 0),
            out_shape=jax.ShapeDtypeStruct((T, BR), bf16 if final else f32),
            scratch_shapes=[pltpu.VMEM((HEADS, DH, DH), f32)],
            compiler_params=_params(("arbitrary", "arbitrary")),
            name="ret_fwd" if final else "ret_bwd",
        )(*args)

    rb = call(True, False, None)
    return call(False, True, rb)


def _gla_kernel(rev, final, *refs):
    if final:
        qk_ref, v_ref, small_ref, wa_ref, ba_ref, g_ref, ob_ref, ng_ref, o_ref, st_ref = refs
    else:
        qk_ref, v_ref, small_ref, wa_ref, ba_ref, o_ref, st_ref = refs
    C = CHUNK

    @pl.when(pl.program_id(1) == 0)
    def _():
        st_ref[...] = jnp.zeros_like(st_ref)

    q = qk_ref[:, :GLA_KW] * (GLA_DK ** -0.5)
    k = qk_ref[:, GLA_KW:]
    v = v_ref[...]
    tri = _tri(C, rev)
    log_a = _log_sigmoid(_bdot(small_ref[...], wa_ref[...]) + ba_ref[...]) / GLA_GATE_NORM
    b = _cumsum01(tri.astype(bf16), log_a)
    ref_row = b[C // 2:C // 2 + 1, :]
    b_end = b[0:1, :] if rev else b[C - 1:C, :]
    qt = q * jnp.exp(b - ref_row)
    kt = k * jnp.exp(ref_row - b)
    inter = _bdot_nt(q * jnp.exp(b), st_ref[...])
    lane = lax.broadcasted_iota(jnp.int32, (1, GLA_KW), 1)
    for h in range(HEADS):
        sl = slice(h * DH, (h + 1) * DH)
        a = _bdot_nt(jnp.where(lane // GLA_DK == h, qt, 0.0), kt)
        o = _bdot(jnp.where(tri, a, 0.0), v[:, sl]) + inter[:, sl]
        if final:
            o = _head_norm(o + ob_ref[:, sl], center=False) * ng_ref[...]
            o_ref[:, sl] = (o * _silu(g_ref[:, sl])).astype(o_ref.dtype)
        else:
            o_ref[:, sl] = o
    ee = lax.broadcasted_iota(jnp.int32, (BR, GLA_KW), 0) // DH
    dd = lax.broadcasted_iota(jnp.int32, (BR, GLA_KW), 1) // GLA_DK
    upd = _bdot_tn(v, k * jnp.exp(b_end - b))
    st_ref[...] = st_ref[...] * jnp.exp(b_end) + jnp.where(ee == dd, upd, 0.0)


def _gla_branch(proj, wa_pad, ba, norm_gain, B, L):
    T = B * L
    nc = L // CHUNK

    def call(rev, final, extra):
        d = 1 if rev else 0
        in_specs = [_tok_spec(nc, rev, BLK_GLA_QK), _tok_spec(nc, rev, BLK_GLA_V),
                    _tok_spec(nc, rev, SMALL_BLK, 128),
                    _const_spec((128, GLA_KW)), _const_spec((1, GLA_KW))]
        args = [proj, proj, proj, wa_pad[d], ba[d:d + 1]]
        if final:
            in_specs += [_tok_spec(nc, rev, BLK_GLA_G), _tok_spec(nc, rev, 0), _const_spec((1, DH))]
            args += [proj, extra, norm_gain]
        return pl.pallas_call(
            functools.partial(_gla_kernel, rev, final),
            grid=(B, nc),
            in_specs=in_specs,
            out_specs=_tok_spec(nc, rev, 0),
            out_shape=jax.ShapeDtypeStruct((T, BR), bf16 if final else f32),
            scratch_shapes=[pltpu.VMEM((BR, GLA_KW), f32)],
            compiler_params=_params(("arbitrary", "arbitrary")),
            name="gla_fwd" if final else "gla_bwd",
        )(*args)

    ob = call(True, False, None)
    return call(False, True, ob)


def _ml_kernel(rev, final, *refs):
    if final:
        (u_ref, up_ref, un_ref, small_ref, cw_ref, cb_ref, wq_ref, wk_ref, wv_ref, bias_ref,
         zg_ref, op_ref, hb_ref, ng_ref, skip_ref, o_ref, s_ref, n_ref, m_ref) = refs
    else:
        (u_ref, up_ref, un_ref, small_ref, cw_ref, cb_ref, wq_ref, wk_ref, wv_ref, bias_ref,
         o_ref, s_ref, n_ref, m_ref) = refs
    C = CHUNK
    c = pl.program_id(1)
    nc = pl.num_programs(1)

    @pl.when(c == 0)
    def _():
        s_ref[...] = jnp.zeros_like(s_ref)
        n_ref[...] = jnp.zeros_like(n_ref)
        m_ref[...] = jnp.full(m_ref.shape, NEG, f32)

    pos = (nc - 1 - c) if rev else c
    u = u_ref[...]
    xc = _silu(_conv3(u, up_ref, un_ref, pos > 0, pos < nc - 1, cw_ref[...], cb_ref[...]))
    q = _bdot(xc, wq_ref[...])
    k = _bdot(xc, wk_ref[...]) * (DH ** -0.5)
    v = _bdot(u, wv_ref[...])

    gates = small_ref[...] + bias_ref[...]
    log_f = _log_sigmoid(gates)
    tri = _tri(C, rev)
    cum = _cumsum01(tri.astype(bf16), log_f)
    gates_t = gates.T
    cum_t = cum.T
    d = 1 if rev else 0
    for h in range(HEADS):
        sl = slice(h * DH, (h + 1) * DH)
        li = LANE_ML_I + 4 * d + h
        lf = LANE_ML_F + 4 * d + h
        b_col = cum[:, lf:lf + 1]
        b_row = cum_t[lf:lf + 1, :]
        i_col = gates[:, li:li + 1]
        i_row = gates_t[li:li + 1, :]
        b_end = b_col[0:1, :] if rev else b_col[C - 1:C, :]
        m_prev = m_ref[h][0:1, 0:1]
        s_prev = s_ref[h]
        n_prev = n_ref[h][0:1, :]
        qh = q[:, sl]
        kh = k[:, sl]
        vh = v[:, sl]
        dmat = jnp.where(tri, b_col - b_row + i_row, NEG)
        m_in = b_col + m_prev
        m_t = jnp.maximum(m_in, jnp.max(dmat, axis=1, keepdims=True))
        p = jnp.exp(dmat - m_t) * _bdot_nt(qh, kh)
        a_in = jnp.exp(m_in - m_t)
        num = _bdot(p, vh) + a_in * _bdot(qh, s_prev)
        den = jnp.sum(p, axis=1, keepdims=True) + a_in * jnp.sum(qh * n_prev, axis=1, keepdims=True)
        hd = num / jnp.maximum(jnp.abs(den), jnp.exp(-m_t))
        g_col = b_end - b_col + i_col
        m_loc = jnp.max(g_col, axis=0, keepdims=True)
        kw = kh * jnp.exp(g_col - m_loc)
        m_new = jnp.maximum(b_end + m_prev, m_loc)
        a_old = jnp.exp(b_end + m_prev - m_new)
        a_new = jnp.exp(m_loc - m_new)
        s_ref[h] = a_old * s_prev + a_new * _bdot_tn(kw, vh)
        n_ref[h] = jnp.broadcast_to(a_old * n_prev + a_new * jnp.sum(kw, axis=0, keepdims=True), (8, DH))
        m_ref[h] = jnp.broadcast_to(m_new, (8, DH))
        if final:
            hh = _sigmoid(op_ref[:, sl]) * (hd + hb_ref[:, sl])
            hh = _head_norm(hh, center=True) * ng_ref[:, sl] + skip_ref[:, sl] * xc[:, sl]
            o_ref[:, sl] = (hh * _silu(zg_ref[:, sl])).astype(o_ref.dtype)
        else:
            o_ref[:, sl] = hd


def _ml_branch(proj, conv_w, conv_b, wq, wk, wv, bias_small, norm_gain, skip, B, L):
    T = B * L
    nc = L // CHUNK
    rows8 = CHUNK // 8
    last8 = T // 8 - 1

    def call(rev, final, extra):
        row = _chunk_row(nc, rev)
        prev_spec = pl.BlockSpec((8, BR), lambda b, c: (jnp.maximum(row(b, c) * rows8 - 1, 0), BLK_ML))
        next_spec = pl.BlockSpec((8, BR), lambda b, c: (jnp.minimum((row(b, c) + 1) * rows8, last8), BLK_ML))
        in_specs = [_tok_spec(nc, rev, BLK_ML), prev_spec, next_spec, _tok_spec(nc, rev, SMALL_BLK, 128),
                    _const_spec((3, BR)), _const_spec((1, BR)),
                    _const_spec((BR, BR)), _const_spec((BR, BR)), _const_spec((BR, BR)),
                    _const_spec((1, 128))]
        args = [proj, proj, proj, proj, conv_w, conv_b, wq, wk, wv, bias_small]
        if final:
            in_specs += [_tok_spec(nc, rev, BLK_ML + 1), _tok_spec(nc, rev, BLK_ML + 2), _tok_spec(nc, rev, 0),
                         _const_spec((1, BR)), _const_spec((1, BR))]
            args += [proj, proj, extra, norm_gain, skip]
        return pl.pallas_call(
            functools.partial(_ml_kernel, rev, final),
            grid=(B, nc),
            in_specs=in_specs,
            out_specs=_tok_spec(nc, rev, 0),
            out_shape=jax.ShapeDtypeStruct((T, BR), bf16 if final else f32),
            scratch_shapes=[pltpu.VMEM((HEADS, DH, DH), f32), pltpu.VMEM((HEADS, 8, DH), f32),
                            pltpu.VMEM((HEADS, 8, DH), f32)],
            compiler_params=_params(("arbitrary", "arbitrary")),
            name="ml_fwd" if final else "ml_bwd",
        )(*args)

    hb = call(True, False, None)
    return call(False, True, hb)


def _hy_prep_kernel(x_ref, xp_ref, xn_ref, w_ref, b_ref, o_ref):
    r = pl.program_id(1)
    nr = pl.num_programs(1)
    o_ref[...] = _conv3(x_ref[...], xp_ref, xn_ref, r > 0, r < nr - 1, w_ref[0], b_ref[0])


def _hy_prep(proj, conv_w, conv_b, B, L):
    T = B * L
    tr = 512
    nr = L // tr
    rows8 = tr // 8
    last8 = T // 8 - 1

    def one(s):
        return pl.pallas_call(
            _hy_prep_kernel,
            grid=(B, nr),
            in_specs=[
                pl.BlockSpec((tr, BR), lambda b, r: (b * nr + r, BLK_HY + s)),
                pl.BlockSpec((8, BR), lambda b, r: (jnp.maximum((b * nr + r) * rows8 - 1, 0), BLK_HY + s)),
                pl.BlockSpec((8, BR), lambda b, r: (jnp.minimum((b * nr + r + 1) * rows8, last8), BLK_HY + s)),
                pl.BlockSpec((1, 3, BR), lambda b, r: (s, 0, 0)),
                pl.BlockSpec((1, 1, BR), lambda b, r: (s, 0, 0)),
            ],
            out_specs=pl.BlockSpec((tr, BR), lambda b, r: (b * nr + r, 0)),
            out_shape=jax.ShapeDtypeStruct((T, BR), f32),
            compiler_params=_params(("arbitrary", "arbitrary")),
            name="hy_prep",
        )(proj, proj, proj, conv_w, conv_b)

    return one(0), one(1), one(2)


def _hy_filt_kernel(emb_ref, w1_ref, b1_ref, w2_ref, b2_ref, fr_ref, w3_ref, dl_ref, taps_ref, sc_ref):
    r = pl.program_id(0)

    @pl.when(r == 0)
    def _():
        sc_ref[...] = jnp.zeros_like(sc_ref)

    emb = emb_ref[...]
    fr = fr_ref[...]
    hmid = jnp.sin(fr * (_dot3(emb, w1_ref[...]) + b1_ref[...]))
    hmid = jnp.sin(fr * (_dot3(hmid, w2_ref[...]) + b2_ref[...]))
    hout = _dot3(hmid, w3_ref[0])
    t = emb[:, 0:1]
    valid = emb[:, HY_EMB:HY_EMB + 1]
    taps = hout * jnp.exp(-t * dl_ref[...]) * valid
    taps_ref[...] = taps
    sc_ref[...] += jnp.sum(taps * taps, axis=0, keepdims=True)

    @pl.when(r == pl.num_programs(0) - 1)
    def _():
        sc_ref[...] = lax.rsqrt(sc_ref[...] + EPS)


def _hy_filter(emb, w1p, b1, w2, b2, freq, w3d, deltas2, L):
    n = 2 * L
    tr = 512
    half = L // tr
    return pl.pallas_call(
        _hy_filt_kernel,
        grid=(n // tr,),
        in_specs=[
            pl.BlockSpec((tr, HY_EMB_PAD), lambda r: (r, 0)),
            pl.BlockSpec((HY_EMB_PAD, HY_HIDDEN), lambda r: (0, 0)),
            pl.BlockSpec((1, HY_HIDDEN), lambda r: (0, 0)),
            pl.BlockSpec((HY_HIDDEN, HY_HIDDEN), lambda r: (0, 0)),
            pl.BlockSpec((1, HY_HIDDEN), lambda r: (0, 0)),
            pl.BlockSpec((1, HY_HIDDEN), lambda r: (0, 0)),
            pl.BlockSpec((1, HY_HIDDEN, 2 * BR), lambda r: (r // half, 0, 0)),
            pl.BlockSpec((1, 2 * BR), lambda r: (0, 0)),
        ],
        out_specs=[pl.BlockSpec((tr, 2 * BR), lambda r: (r, 0)),
                   pl.BlockSpec((1, 2 * BR), lambda r: (0, 0))],
        out_shape=[jax.ShapeDtypeStruct((n, 2 * BR), f32), jax.ShapeDtypeStruct((1, 2 * BR), f32)],
        compiler_params=_params(("arbitrary",)),
        name="hy_filter",
    )(emb, w1p, b1, w2, b2, freq, w3d, deltas2)


def _dft_outer_kernel(m_ref, x_ref, o_ref):
    o_ref[0] = jnp.dot(m_ref[...], x_ref[0].astype(bf16), preferred_element_type=f32)


def _dft_outer(mat, x3):
    B, K, W = x3.shape
    M = mat.shape[0]
    tl = FFT_LANES
    return pl.pallas_call(
        _dft_outer_kernel,
        grid=(B, W // tl),
        in_specs=[pl.BlockSpec((M, K), lambda b, t: (0, 0)),
                  pl.BlockSpec((1, K, tl), lambda b, t: (b, 0, t))],
        out_specs=pl.BlockSpec((1, M, tl), lambda b, t: (b, 0, t)),
        out_shape=jax.ShapeDtypeStruct((B, M, W), f32),
        compiler_params=_params(("arbitrary", "arbitrary")),
        name="dft_outer",
    )(mat, x3)


def _dft_inner_kernel(conv, *refs):
    if conv:
        a_ref, gf_ref, h_ref, gi_ref, o_ref = refs
    else:
        a_ref, gf_ref, o_ref = refs
    n2 = FFT_N2
    for kk in range(FFT_K1T):
        a = a_ref[0, kk].astype(bf16)
        x = jnp.dot(gf_ref[kk], a, preferred_element_type=f32)
        if conv:
            xr, xi = x[:n2], x[n2:]
            hr, hi = h_ref[0, kk, :n2], h_ref[0, kk, n2:]
            yr = (xr * hr - xi * hi).astype(bf16)
            yi = (xr * hi + xi * hr).astype(bf16)
            gi = gi_ref[kk]
            x = (jnp.dot(gi[:, :n2], yr, preferred_element_type=f32)
                 + jnp.dot(gi[:, n2:], yi, preferred_element_type=f32))
        o_ref[0, kk] = x


def _dft_inner(a4, gf, h4=None, gi=None, hblk=0):
    B, n1, r2, W = a4.shape
    conv = h4 is not None
    kt = FFT_K1T
    nw = W // BR
    in_specs = [pl.BlockSpec((1, kt, r2, BR), lambda k, w, b: (b, k, 0, w)),
                pl.BlockSpec((kt, r2, r2), lambda k, w, b: (k, 0, 0))]
    args = [a4, gf]
    if conv:
        in_specs += [pl.BlockSpec((1, kt, r2, BR), lambda k, w, b: (0, k, 0, hblk)),
                     pl.BlockSpec((kt, r2, r2), lambda k, w, b: (k, 0, 0))]
        args += [h4, gi]
    return pl.pallas_call(
        functools.partial(_dft_inner_kernel, conv),
        grid=(n1 // kt, nw, B),
        in_specs=in_specs,
        out_specs=pl.BlockSpec((1, kt, r2, BR), lambda k, w, b: (b, k, 0, w)),
        out_shape=jax.ShapeDtypeStruct((B, n1, r2, W), f32),
        compiler_params=_params(("arbitrary", "arbitrary", "arbitrary")),
        name="dft_inner_conv" if conv else "dft_inner_spec",
    )(*args)


def _idft_outer_kernel(final, *refs):
    if final:
        m_ref, b_ref, mul_ref, s_ref, sc_ref, sk_ref, g_ref, o_ref = refs
    else:
        m_ref, b_ref, mul_ref, s_ref, sc_ref, sk_ref, o_ref = refs
    y = jnp.dot(m_ref[...], b_ref[0].astype(bf16), preferred_element_type=f32)
    z = mul_ref[0] * (y * sc_ref[...] + s_ref[0] * sk_ref[...])
    if final:
        z = z * _silu(g_ref[0])
    o_ref[0] = z.astype(o_ref.dtype)


def _idft_outer(mat, b3, mul3, s3, scale_t, skip_t, g3=None):
    B, M, W = b3.shape
    K = mat.shape[0]
    tl = FFT_LANES
    final = g3 is not None
    tile = pl.BlockSpec((1, K, tl), lambda b, t: (b, 0, t))
    vec = pl.BlockSpec((1, tl), lambda b, t: (0, 0))
    in_specs = [pl.BlockSpec((K, M), lambda b, t: (0, 0)),
                pl.BlockSpec((1, M, tl), lambda b, t: (b, 0, t)), tile, tile, vec, vec]
    args = [mat, b3, mul3, s3, scale_t, skip_t]
    if final:
        in_specs.append(tile)
        args.append(g3)
    return pl.pallas_call(
        functools.partial(_idft_outer_kernel, final),
        grid=(B, W // tl),
        in_specs=in_specs,
        out_specs=tile,
        out_shape=jax.ShapeDtypeStruct((B, K, W), bf16 if final else f32),
        compiler_params=_params(("arbitrary", "arbitrary")),
        name="idft_outer",
    )(*args)


def _fft_tables(L):
    n = 2 * L
    n2 = FFT_N2
    n1 = n // n2
    kh = n1 // 2
    two_pi = 2.0 * math.pi
    k1 = jnp.arange(n1, dtype=jnp.int32)[:, None]
    m1 = jnp.arange(kh, dtype=jnp.int32)[None, :]
    ang = (two_pi / n1) * ((k1 * m1) % n1).astype(f32)
    f_half = jnp.stack([jnp.cos(ang), -jnp.sin(ang)], axis=1).reshape(2 * n1, kh).astype(bf16)
    mf = jnp.arange(n1, dtype=jnp.int32)[None, :]
    angf = (two_pi / n1) * ((k1 * mf) % n1).astype(f32)
    f_full = jnp.stack([jnp.cos(angf), -jnp.sin(angf)], axis=1).reshape(2 * n1, n1).astype(bf16)
    r1 = jnp.arange(kh, dtype=jnp.int32)[:, None]
    angi = (two_pi / n1) * ((r1 * mf) % n1).astype(f32)
    f_inv = jnp.stack([jnp.cos(angi), -jnp.sin(angi)], axis=2).reshape(kh, 2 * n1).astype(bf16)
    kk = jnp.arange(n1, dtype=jnp.int32)[:, None, None]
    a = jnp.arange(n2, dtype=jnp.int32)[None, :, None]
    c = jnp.arange(n2, dtype=jnp.int32)[None, None, :]
    pf = (a * c * n1 + c * kk) % n
    angp = (two_pi / n) * pf.astype(f32)
    mr, mi = jnp.cos(angp), -jnp.sin(angp)
    g_fwd = jnp.concatenate([jnp.concatenate([mr, -mi], axis=2),
                             jnp.concatenate([mi, mr], axis=2)], axis=1).astype(bf16)
    pi_ = (a * c * n1 + a * kk) % n
    angq = (two_pi / n) * pi_.astype(f32)
    vr, vi = jnp.cos(angq), jnp.sin(angq)
    g_inv = jnp.concatenate([jnp.concatenate([vr, -vi], axis=2),
                             jnp.concatenate([vi, vr], axis=2)], axis=1).astype(bf16)
    return dict(n=n, n1=n1, kh=kh, f_half=f_half, f_full=f_full, f_inv=f_inv, g_fwd=g_fwd, g_inv=g_inv)


def _hy_embedding(L):
    n = jnp.arange(2 * L, dtype=jnp.int32)
    pos = jnp.where(n < L, n, 2 * L - n).astype(f32)
    t = pos / (L - 1)
    f = jnp.linspace(1e-4, HY_BANDS - 1, HY_BANDS, dtype=f32)
    ang = (2.0 * math.pi / L) * pos[:, None] * f[None, :]
    valid = (n != L).astype(f32)
    pad = jnp.zeros((2 * L, HY_EMB_PAD - HY_EMB - 1), f32)
    return jnp.concatenate([t[:, None], jnp.cos(ang), -jnp.sin(ang), valid[:, None], pad], axis=-1)


def _hy_branch(proj, lw, emb, tabs, B, L):
    T = B * L
    n2 = FFT_N2
    n1, kh = tabs["n1"], tabs["kh"]
    W = n2 * BR
    taps, scale = _hy_filter(emb, lw["hy_w1"], lw["hy_b1"], lw["hy_w2"], lw["hy_b2"], lw["hy_freq"],
                             lw["hy_w3"], lw["hy_deltas"], L)
    ta = _dft_outer(tabs["f_full"], taps.reshape(1, n1, n2 * 2 * BR))
    h4 = _dft_inner(ta.reshape(1, n1, 2 * n2, 2 * BR), tabs["g_fwd"])
    v, x1, x2 = _hy_prep(proj, lw["hy_conv_w"], lw["hy_conv_b"], B, L)
    reps = FFT_LANES // BR
    inv_n = 1.0 / tabs["n"]

    def conv(s, order):
        a = _dft_outer(tabs["f_half"], s.reshape(B, kh, W))
        b4 = _dft_inner(a.reshape(B, n1, 2 * n2, BR), tabs["g_fwd"], h4, tabs["g_inv"], order)
        return b4.reshape(B, 2 * n1, W)

    def vecs(order):
        sc = jnp.tile(scale[:, order * BR:(order + 1) * BR] * inv_n, (1, reps))
        sk = jnp.tile(lw["hy_skip"][order:order + 1], (1, reps))
        return sc, sk

    sc0, sk0 = vecs(0)
    z = _idft_outer(tabs["f_inv"], conv(v, 0), x1.reshape(B, kh, W), v.reshape(B, kh, W), sc0, sk0)
    sc1, sk1 = vecs(1)
    g = lax.slice_in_dim(proj, (BLK_HY + 3) * BR, (BLK_HY + 4) * BR, axis=1)
    out = _idft_outer(tabs["f_inv"], conv(z.reshape(T, BR), 1), x2.reshape(B, kh, W), z, sc1, sk1,
                      g.reshape(B, kh, W))
    return out.reshape(T, BR)


def _rotary_tables(L):
    inv = ROPE_BASE ** (-jnp.arange(0, DH, 2, dtype=f32) / DH)
    ang = jnp.arange(L, dtype=f32)[:, None] * inv[None, :]
    cos, sin = jnp.cos(ang), jnp.sin(ang)
    return jnp.concatenate([cos, cos], axis=-1), jnp.concatenate([-sin, sin], axis=-1)


def _block_diag(w):
    n, e, fo = w.shape
    eye = jnp.eye(n, dtype=w.dtype)
    return (eye[:, None, :, None] * w[:, :, None, :]).reshape(n * e, n * fo)


def _permute_w_in(w_in):
    ref_gla_small = 2048 + 1536
    ref_hy = ref_gla_small + 2 * GLA_RANK
    ref_ml = ref_hy + 2048
    ref_ml_small = ref_ml + 1536
    parts = [w_in[..., :ref_gla_small], w_in[..., ref_hy:ref_ml], w_in[..., ref_ml:ref_ml_small],
             w_in[..., ref_gla_small:ref_hy], w_in[..., ref_ml_small:]]
    w = jnp.concatenate(parts, axis=-1)
    return jnp.pad(w, ((0, 0), (0, 0), (0, N_PROJ - w.shape[-1]))).astype(bf16)


def _layer_weights(l, p):
    lw = {}
    lw["gla_wa"] = jnp.stack([
        jnp.pad(p["gla_w_a"][l, d], ((LANE_GLA_A + GLA_RANK * d, 128 - LANE_GLA_A - GLA_RANK * (d + 1)), (0, 0)))
        for d in range(2)]).astype(bf16)
    lw["gla_ba"] = p["gla_b_a"][l]
    lw["gla_ng"] = p["gla_norm_gain"][l].reshape(1, DH)
    lw["hy_conv_w"] = p["hy_conv_w"][l].reshape(3, 3, BR).transpose(1, 0, 2)
    lw["hy_conv_b"] = p["hy_conv_b"][l].reshape(3, 1, BR)
    lw["hy_w1"] = jnp.pad(p["hy_w1"][l], ((0, HY_EMB_PAD - HY_EMB), (0, 0)))
    lw["hy_b1"] = p["hy_b1"][l].reshape(1, HY_HIDDEN)
    lw["hy_w2"] = p["hy_w2"][l]
    lw["hy_b2"] = p["hy_b2"][l].reshape(1, HY_HIDDEN)
    lw["hy_freq"] = p["hy_freq"][l].reshape(1, HY_HIDDEN)
    lw["hy_w3"] = p["hy_w3"][l].reshape(HY_HIDDEN, 2, 2, BR).transpose(2, 0, 1, 3).reshape(2, HY_HIDDEN, 2 * BR)
    deltas = jnp.abs(jnp.linspace(HY_MIN_DECAY, HY_MAX_DECAY, BR, dtype=f32))
    lw["hy_deltas"] = jnp.concatenate([deltas, deltas]).reshape(1, 2 * BR)
    lw["hy_skip"] = p["hy_skip"][l]
    lw["ml_conv_w"] = p["ml_conv_w"][l]
    lw["ml_conv_b"] = p["ml_conv_b"][l].reshape(1, BR)
    lw["ml_wq"] = _block_diag(p["ml_wq"][l]).astype(bf16)
    lw["ml_wk"] = _block_diag(p["ml_wk"][l]).astype(bf16)
    lw["ml_wv"] = _block_diag(p["ml_wv"][l]).astype(bf16)
    bias = jnp.concatenate([p["ml_b_i"][l].reshape(-1), p["ml_b_f"][l].reshape(-1)])
    lw["ml_bias"] = jnp.pad(bias, (LANE_ML_I, 128 - LANE_ML_I - bias.shape[0])).reshape(1, 128)
    lw["ml_ng"] = p["ml_norm_gain"][l].reshape(1, BR)
    lw["ml_skip"] = p["ml_skip"][l].reshape(1, BR)
    return lw


def _mix(proj, lw, consts, B, L):
    cosf, sins, emb, tabs = consts
    return (
        _ret_branch(proj, cosf, sins, B, L),
        _gla_branch(proj, lw["gla_wa"], lw["gla_ba"], lw["gla_ng"], B, L),
        _hy_branch(proj, lw, emb, tabs, B, L),
        _ml_branch(proj, lw["ml_conv_w"], lw["ml_conv_b"], lw["ml_wq"], lw["ml_wk"], lw["ml_wv"],
                   lw["ml_bias"], lw["ml_ng"], lw["ml_skip"], B, L),
    )


def kernel(x_prompt, x_sample, c_prompt, c_sample, norm_gain, w_ada, b_ada, w_in, gla_w_a, gla_b_a, gla_norm_gain, hy_conv_w, hy_conv_b, hy_w1, hy_b1, hy_w2, hy_b2, hy_freq, hy_w3, hy_skip, ml_conv_w, ml_conv_b, ml_wq, ml_wk, ml_wv, ml_b_i, ml_b_f, ml_norm_gain, ml_skip, w_branch, w_gate, b_gate, w_out, final_gain):
    p = dict(gla_w_a=gla_w_a, gla_b_a=gla_b_a, gla_norm_gain=gla_norm_gain, hy_conv_w=hy_conv_w,
             hy_conv_b=hy_conv_b, hy_w1=hy_w1, hy_b1=hy_b1, hy_w2=hy_w2, hy_b2=hy_b2, hy_freq=hy_freq,
             hy_w3=hy_w3, hy_skip=hy_skip, ml_conv_w=ml_conv_w, ml_conv_b=ml_conv_b, ml_wq=ml_wq,
             ml_wk=ml_wk, ml_wv=ml_wv, ml_b_i=ml_b_i, ml_b_f=ml_b_f, ml_norm_gain=ml_norm_gain,
             ml_skip=ml_skip)
    groups = []
    for x, c in ((x_prompt, c_prompt), (x_sample, c_sample)):
        B, L, _ = x.shape
        groups.append(dict(B=B, L=L, x=x.reshape(B * L, D_MODEL),
                           consts=_rotary_tables(L) + (_hy_embedding(L), _fft_tables(L))))
    nb = [g["B"] for g in groups]
    rows = -(-sum(nb) // 8) * 8
    c_all = jnp.concatenate([c_prompt, c_sample, jnp.zeros((rows - sum(nb), D_MODEL), f32)], axis=0)
    mod_all = _ada_all(c_all, w_ada, b_ada)

    w_in_p = _permute_w_in(w_in)
    wg = w_gate.astype(bf16)
    wb = w_branch.astype(bf16)
    wo = w_out.astype(bf16)
    fg = final_gain.reshape(1, D_MODEL)
    for l in range(DEPTH):
        lw = _layer_weights(l, p)
        gain = norm_gain[l].reshape(1, D_MODEL)
        bg = b_gate[l].reshape(4, 1, D_MODEL)
        start = 0
        for g in groups:
            B, L = g["B"], g["L"]
            mod = mod_all[l, start:start + B].reshape(B, 3, D_MODEL)
            start += B
            proj = _inproj(g["x"], mod, gain, w_in_p[l], L)
            branches = _mix(proj, lw, g["consts"], B, L)
            g["x"] = _merge(g["x"], mod, gain, branches, wg[l], bg, wb[l], wo[l], fg, L, l == DEPTH - 1)
    return tuple(g["x"].reshape(g["B"], g["L"], D_MODEL) for g in groups)
```

```python
import functools
import math

import jax
import jax.numpy as jnp
from jax import lax
from jax.experimental import pallas as pl
from jax.experimental.pallas import tpu as pltpu

f32 = jnp.float32
bf16 = jnp.bfloat16

D_MODEL = 2048
DEPTH = 4
BR = 512
HEADS = 4
DH = 128
ROPE_BASE = 10000.0
GLA_DK = 64
GLA_KW = 256
GLA_RANK = 16
GLA_GATE_NORM = 16.0
HY_EMB = 33
HY_EMB_PAD = 40
HY_BANDS = 16
HY_HIDDEN = 64
HY_MIN_DECAY = math.log(1e-2) / 1.5
HY_MAX_DECAY = math.log(1e-2) / 0.3
ML_BLOCK = 4
EPS = 1e-6
NEG = -1e30

BLK_RET = 0
BLK_GLA_QK = 4
BLK_GLA_V = 5
BLK_GLA_G = 6
BLK_HY = 7
BLK_ML = 11
N_PROJ = 7168
N_SMALL = 128
LANE_GLA_A = 0
LANE_ML_I = 32
LANE_ML_F = 40

CHUNK = 128
HALO = 16
FFT_N2 = 128
FFT_K1T = 8
FFT_NJ = 4
PITCH_X = FFT_N2 + 8
PITCH_A = 2 * FFT_N2 + 8
TM_DENSE = 1024
TN_IN = 1792
TN_MERGE = 256
VMEM_LIMIT = 60 * 1024 * 1024

LOG_GAMMA = tuple(math.log(1.0 - 2.0 ** (-5.0 - h)) for h in range(HEADS))


def _params(sem):
    return pltpu.CompilerParams(dimension_semantics=sem, vmem_limit_bytes=VMEM_LIMIT)


def _bdot(a, b):
    return jnp.dot(a.astype(bf16), b.astype(bf16), preferred_element_type=f32)


def _bdot_nt(a, b):
    return lax.dot_general(a.astype(bf16), b.astype(bf16), (((1,), (1,)), ((), ())),
                           preferred_element_type=f32)


def _bdot_tn(a, b):
    return jnp.dot(a.T.astype(bf16), b.astype(bf16), preferred_element_type=f32)


def _dot3(a, b):
    a1 = a.astype(bf16)
    ra = a - a1.astype(f32)
    a2 = ra.astype(bf16)
    a3 = (ra - a2.astype(f32)).astype(bf16)
    b1 = b.astype(bf16)
    rb = b - b1.astype(f32)
    b2 = rb.astype(bf16)
    b3 = (rb - b2.astype(f32)).astype(bf16)
    d = functools.partial(jnp.dot, preferred_element_type=f32)
    return (d(a1, b1) + (d(a1, b2) + d(a2, b1))) + ((d(a2, b2) + d(a1, b3)) + d(a3, b1))


def _cumsum01(tri, x):
    x1 = x.astype(bf16)
    r = x - x1.astype(f32)
    x2 = r.astype(bf16)
    x3 = (r - x2.astype(f32)).astype(bf16)
    d = functools.partial(jnp.dot, preferred_element_type=f32)
    return d(tri, x1) + (d(tri, x2) + d(tri, x3))


def _sigmoid(x):
    return jax.nn.sigmoid(x)


def _silu(x):
    return x * jax.nn.sigmoid(x)


def _log_sigmoid(x):
    return jnp.minimum(x, 0.0) - jnp.log1p(jnp.exp(-jnp.abs(x)))


def _tri(n, rev):
    ii = lax.broadcasted_iota(jnp.int32, (n, n), 0)
    jj = lax.broadcasted_iota(jnp.int32, (n, n), 1)
    return (jj >= ii) if rev else (jj <= ii)


def _prev_rows(x, first):
    row = lax.broadcasted_iota(jnp.int32, x.shape, 0)
    return jnp.where(row == 0, first, pltpu.roll(x, 1, 0))


def _next_rows(x, last):
    n = x.shape[0]
    row = lax.broadcasted_iota(jnp.int32, x.shape, 0)
    return jnp.where(row == n - 1, last, pltpu.roll(x, n - 1, 0))


def _conv3(x, prev, nxt, has_prev, has_next, w, b):
    first = jnp.where(has_prev, prev[HALO - 1:HALO, :], 0.0)
    last = jnp.where(has_next, nxt[0:1, :], 0.0)
    return _prev_rows(x, first) * w[0:1] + x * w[1:2] + _next_rows(x, last) * w[2:3] + b


def _head_norm(x, center):
    if center:
        x = x - jnp.mean(x, axis=-1, keepdims=True)
    return x * lax.rsqrt(jnp.mean(x * x, axis=-1, keepdims=True) + EPS)


def _ada_kernel(c_ref, w_ref, b_ref, o_ref):
    o_ref[0] = _bdot(_silu(c_ref[...]), w_ref[0]) + b_ref[0]


def _ada_all(c_all, w_ada, b_ada):
    rows = c_all.shape[0]
    tn = 1536
    return pl.pallas_call(
        _ada_kernel,
        grid=(DEPTH, 3 * D_MODEL // tn),
        in_specs=[
            pl.BlockSpec((rows, D_MODEL), lambda l, j: (0, 0)),
            pl.BlockSpec((1, D_MODEL, tn), lambda l, j: (l, 0, j)),
            pl.BlockSpec((1, 1, tn), lambda l, j: (l, 0, j)),
        ],
        out_specs=pl.BlockSpec((1, rows, tn), lambda l, j: (l, 0, j)),
        out_shape=jax.ShapeDtypeStruct((DEPTH, rows, 3 * D_MODEL), f32),
        compiler_params=_params(("arbitrary", "arbitrary")),
        name="ada",
    )(c_all, w_ada, b_ada.reshape(DEPTH, 1, 3 * D_MODEL))


ROWS_NORM = 256


def _modulated_norm(x, mod, gain):
    y = x * lax.rsqrt(jnp.mean(x * x, axis=-1, keepdims=True) + EPS) * gain
    return y * (1.0 + mod[1:2]) + mod[0:1]


def _norm_to_scratch(x_ref, mod_ref, gain_ref, h_ref):
    mod = mod_ref[0]
    gain = gain_ref[...]

    def step(i, carry):
        rows = pl.ds(pl.multiple_of(i * ROWS_NORM, ROWS_NORM), ROWS_NORM)
        h_ref[rows, :] = _modulated_norm(x_ref[rows, :], mod, gain).astype(bf16)
        return carry

    lax.fori_loop(0, x_ref.shape[0] // ROWS_NORM, step, 0)


def _inproj_kernel(x_ref, mod_ref, gain_ref, w_ref, ws_ref, o_ref, os_ref, h_ref):
    @pl.when(pl.program_id(1) == 0)
    def _():
        _norm_to_scratch(x_ref, mod_ref, gain_ref, h_ref)
        os_ref[...] = jnp.dot(h_ref[...], ws_ref[...], preferred_element_type=f32)

    o_ref[...] = jnp.dot(h_ref[...], w_ref[...], preferred_element_type=f32).astype(bf16)


def _inproj(x, mod, gain, w, ws, L):
    T = x.shape[0]
    tm = min(TM_DENSE, L)
    return pl.pallas_call(
        _inproj_kernel,
        grid=(T // tm, N_PROJ // TN_IN),
        in_specs=[
            pl.BlockSpec((tm, D_MODEL), lambda i, j: (i, 0)),
            pl.BlockSpec((1, 3, D_MODEL), lambda i, j: ((i * tm) // L, 0, 0)),
            pl.BlockSpec((1, D_MODEL), lambda i, j: (0, 0)),
            pl.BlockSpec((D_MODEL, TN_IN), lambda i, j: (0, j)),
            pl.BlockSpec((D_MODEL, N_SMALL), lambda i, j: (0, 0)),
        ],
        out_specs=[pl.BlockSpec((tm, TN_IN), lambda i, j: (i, j)),
                   pl.BlockSpec((tm, N_SMALL), lambda i, j: (i, 0))],
        out_shape=[jax.ShapeDtypeStruct((T, N_PROJ), bf16), jax.ShapeDtypeStruct((T, N_SMALL), f32)],
        scratch_shapes=[pltpu.VMEM((tm, D_MODEL), bf16)],
        compiler_params=_params(("arbitrary", "arbitrary")),
        name="inproj",
    )(x, mod, gain, w, ws)


def _merge_kernel(final, x_ref, mod_ref, gain_ref, b0_ref, b1_ref, b2_ref, b3_ref,
                  wg_ref, bg_ref, wb_ref, wo_ref, fg_ref, o_ref, h_ref):
    j = pl.program_id(1)

    @pl.when(j == 0)
    def _():
        _norm_to_scratch(x_ref, mod_ref, gain_ref, h_ref)
        o_ref[...] = jnp.zeros_like(o_ref)

    h = h_ref[...]
    merged = None
    for bi, br_ref in enumerate((b0_ref, b1_ref, b2_ref, b3_ref)):
        g = _sigmoid(jnp.dot(h, wg_ref[bi], preferred_element_type=f32) + bg_ref[bi])
        t = g * jnp.dot(br_ref[...], wb_ref[bi], preferred_element_type=f32)
        merged = t if merged is None else merged + t
    o_ref[...] += jnp.dot(merged.astype(bf16), wo_ref[...], preferred_element_type=f32)

    @pl.when(j == pl.num_programs(1) - 1)
    def _():
        gate = mod_ref[0][2:3]

        def step(i, carry):
            rows = pl.ds(pl.multiple_of(i * ROWS_NORM, ROWS_NORM), ROWS_NORM)
            y = x_ref[rows, :] + gate * o_ref[rows, :]
            if final:
                y = y * lax.rsqrt(jnp.mean(y * y, axis=-1, keepdims=True) + EPS) * fg_ref[...]
            o_ref[rows, :] = y
            return carry

        lax.fori_loop(0, x_ref.shape[0] // ROWS_NORM, step, 0)


def _merge(x, mod, gain, branches, wg, bg, wb, wo, final_gain, L, final):
    T = x.shape[0]
    tm = min(TM_DENSE, L)
    tn = TN_MERGE
    br_spec = pl.BlockSpec((tm, BR), lambda i, j: (i, 0))
    return pl.pallas_call(
        functools.partial(_merge_kernel, final),
        grid=(T // tm, D_MODEL // tn),
        in_specs=[
            pl.BlockSpec((tm, D_MODEL), lambda i, j: (i, 0), pipeline_mode=pl.Buffered(1)),
            pl.BlockSpec((1, 3, D_MODEL), lambda i, j: ((i * tm) // L, 0, 0)),
            pl.BlockSpec((1, D_MODEL), lambda i, j: (0, 0)),
            br_spec, br_spec, br_spec, br_spec,
            pl.BlockSpec((4, D_MODEL, tn), lambda i, j: (0, 0, j)),
            pl.BlockSpec((4, 1, tn), lambda i, j: (0, 0, j)),
            pl.BlockSpec((4, BR, tn), lambda i, j: (0, 0, j)),
            pl.BlockSpec((tn, D_MODEL), lambda i, j: (j, 0)),
            pl.BlockSpec((1, D_MODEL), lambda i, j: (0, 0)),
        ],
        out_specs=pl.BlockSpec((tm, D_MODEL), lambda i, j: (i, 0)),
        out_shape=jax.ShapeDtypeStruct((T, D_MODEL), f32),
        scratch_shapes=[pltpu.VMEM((tm, D_MODEL), bf16)],
        compiler_params=_params(("arbitrary", "arbitrary")),
        name="merge",
    )(x, mod, gain, *branches, wg, bg, wb, wo, final_gain)


def _chunk_row(nc, rev):
    if rev:
        return lambda b, c: b * nc + (nc - 1 - c)
    return lambda b, c: b * nc + c


def _tok_spec(nc, rev, blk, width=BR):
    row = _chunk_row(nc, rev)
    return pl.BlockSpec((CHUNK, width), lambda b, c: (row(b, c), blk))


def _pos_spec(nc, rev):
    if rev:
        return pl.BlockSpec((CHUNK, DH), lambda b, c: (nc - 1 - c, 0))
    return pl.BlockSpec((CHUNK, DH), lambda b, c: (c, 0))


def _const_spec(shape):
    nd = len(shape)
    return pl.BlockSpec(shape, lambda b, c: (0,) * nd)


def _ret_kernel(rev, final, *refs):
    if final:
        q_ref, k_ref, v_ref, cos_ref, sin_ref, g_ref, rb_ref, o_ref, s_ref = refs
    else:
        q_ref, k_ref, v_ref, cos_ref, sin_ref, o_ref, s_ref = refs
    C = CHUNK

    @pl.when(pl.program_id(1) == 0)
    def _():
        s_ref[...] = jnp.zeros_like(s_ref)

    cos = cos_ref[...]
    sin = sin_ref[...]
    ii = lax.broadcasted_iota(jnp.int32, (C, C), 0)
    jj = lax.broadcasted_iota(jnp.int32, (C, C), 1)
    col = lax.broadcasted_iota(jnp.int32, (C, 1), 0).astype(f32)
    dist = (ii - jj).astype(f32)
    for h in range(HEADS):
        lg = LOG_GAMMA[h]
        sl = slice(h * DH, (h + 1) * DH)
        q = q_ref[:, sl].astype(f32)
        k = k_ref[:, sl].astype(f32)
        v = v_ref[:, sl]
        q = q * cos + pltpu.roll(q, DH // 2, 1) * sin
        k = (k * cos + pltpu.roll(k, DH // 2, 1) * sin) * (DH ** -0.5)
        if rev:
            gam = jnp.where(jj > ii, jnp.exp(-dist * lg), 0.0)
            qd = jnp.exp((C - col) * lg)
            kd = jnp.exp(col * lg)
        else:
            gam = jnp.where(jj <= ii, jnp.exp(dist * lg), 0.0)
            qd = jnp.exp((col + 1.0) * lg)
            kd = jnp.exp((C - 1.0 - col) * lg)
        s_prev = s_ref[h]
        o = _bdot(_bdot_nt(q, k) * gam, v) + qd * _bdot(q, s_prev)
        s_ref[h] = math.exp(C * lg) * s_prev + _bdot_tn(k * kd, v)
        if final:
            r = _head_norm(o + rb_ref[:, sl], center=True)
            o_ref[:, sl] = (r * _silu(g_ref[:, sl].astype(f32))).astype(o_ref.dtype)
        else:
            o_ref[:, sl] = o


def _ret_branch(proj, cosf, sins, B, L):
    T = B * L
    nc = L // CHUNK

    def call(rev, final, extra):
        in_specs = [_tok_spec(nc, rev, BLK_RET + i) for i in range(3)]
        in_specs += [_pos_spec(nc, rev), _pos_spec(nc, rev)]
        args = [proj, proj, proj, cosf, sins]
        if final:
            in_specs += [_tok_spec(nc, rev, BLK_RET + 3), _tok_spec(nc, rev, 0)]
            args += [proj, extra]
        return pl.pallas_call(
            functools.partial(_ret_kernel, rev, final),
            grid=(B, nc),
            in_specs=in_specs,
            out_specs=_tok_spec(nc, rev, 0),
            out_shape=jax.ShapeDtypeStruct((T, BR), bf16 if final else f32),
            scratch_shapes=[pltpu.VMEM((HEADS, DH, DH), f32)],
            compiler_params=_params(("arbitrary", "arbitrary")),
            name="ret_fwd" if final else "ret_bwd",
        )(*args)

    rb = call(True, False, None)
    return call(False, True, rb)


def _gla_kernel(rev, final, *refs):
    if final:
        qk_ref, v_ref, small_ref, wa_ref, ba_ref, g_ref, ob_ref, ng_ref, o_ref, st_ref = refs
    else:
        qk_ref, v_ref, small_ref, wa_ref, ba_ref, o_ref, st_ref = refs
    C = CHUNK

    @pl.when(pl.program_id(1) == 0)
    def _():
        st_ref[...] = jnp.zeros_like(st_ref)

    q = qk_ref[:, :GLA_KW].astype(f32) * (GLA_DK ** -0.5)
    k = qk_ref[:, GLA_KW:].astype(f32)
    v = v_ref[...].astype(f32)
    tri = _tri(C, rev)
    log_a = _log_sigmoid(_bdot(small_ref[...], wa_ref[...]) + ba_ref[...]) / GLA_GATE_NORM
    b = _cumsum01(tri.astype(bf16), log_a)
    ref_row = b[C // 2:C // 2 + 1, :]
    b_end = b[0:1, :] if rev else b[C - 1:C, :]
    qt = q * jnp.exp(b - ref_row)
    kt = k * jnp.exp(ref_row - b)
    inter = _bdot_nt(q * jnp.exp(b), st_ref[...])
    lane = lax.broadcasted_iota(jnp.int32, (1, GLA_KW), 1)
    for h in range(HEADS):
        sl = slice(h * DH, (h + 1) * DH)
        a = _bdot_nt(jnp.where(lane // GLA_DK == h, qt, 0.0), kt)
        o = _bdot(jnp.where(tri, a, 0.0), v[:, sl]) + inter[:, sl]
        if final:
            o = _head_norm(o + ob_ref[:, sl], center=False) * ng_ref[...]
            o_ref[:, sl] = (o * _silu(g_ref[:, sl].astype(f32))).astype(o_ref.dtype)
        else:
            o_ref[:, sl] = o
    ee = lax.broadcasted_iota(jnp.int32, (BR, GLA_KW), 0) // DH
    dd = lax.broadcasted_iota(jnp.int32, (BR, GLA_KW), 1) // GLA_DK
    upd = _bdot_tn(v, k * jnp.exp(b_end - b))
    st_ref[...] = st_ref[...] * jnp.exp(b_end) + jnp.where(ee == dd, upd, 0.0)


def _gla_branch(proj, small, wa_pad, ba, norm_gain, B, L):
    T = B * L
    nc = L // CHUNK

    def call(rev, final, extra):
        d = 1 if rev else 0
        in_specs = [_tok_spec(nc, rev, BLK_GLA_QK), _tok_spec(nc, rev, BLK_GLA_V),
                    _tok_spec(nc, rev, 0, N_SMALL),
                    _const_spec((128, GLA_KW)), _const_spec((1, GLA_KW))]
        args = [proj, proj, small, wa_pad[d], ba[d:d + 1]]
        if final:
            in_specs += [_tok_spec(nc, rev, BLK_GLA_G), _tok_spec(nc, rev, 0), _const_spec((1, DH))]
            args += [proj, extra, norm_gain]
        return pl.pallas_call(
            functools.partial(_gla_kernel, rev, final),
            grid=(B, nc),
            in_specs=in_specs,
            out_specs=_tok_spec(nc, rev, 0),
            out_shape=jax.ShapeDtypeStruct((T, BR), bf16 if final else f32),
            scratch_shapes=[pltpu.VMEM((BR, GLA_KW), f32)],
            compiler_params=_params(("arbitrary", "arbitrary")),
            name="gla_fwd" if final else "gla_bwd",
        )(*args)

    ob = call(True, False, None)
    return call(False, True, ob)


def _ml_kernel(rev, final, *refs):
    if final:
        (u_ref, up_ref, un_ref, small_ref, cw_ref, cb_ref, wq_ref, wk_ref, wv_ref, bias_ref,
         zg_ref, op_ref, hb_ref, ng_ref, skip_ref, o_ref, s_ref, n_ref, m_ref) = refs
    else:
        (u_ref, up_ref, un_ref, small_ref, cw_ref, cb_ref, wq_ref, wk_ref, wv_ref, bias_ref,
         o_ref, s_ref, n_ref, m_ref) = refs
    C = CHUNK
    c = pl.program_id(1)
    nc = pl.num_programs(1)

    @pl.when(c == 0)
    def _():
        s_ref[...] = jnp.zeros_like(s_ref)
        n_ref[...] = jnp.zeros_like(n_ref)
        m_ref[...] = jnp.full(m_ref.shape, NEG, f32)

    pos = (nc - 1 - c) if rev else c
    u = u_ref[...].astype(f32)
    xc = _silu(_conv3(u, up_ref[...].astype(f32), un_ref[...].astype(f32), pos > 0, pos < nc - 1,
                      cw_ref[...], cb_ref[...]))
    q = _bdot(xc, wq_ref[...])
    k = _bdot(xc, wk_ref[...]) * (DH ** -0.5)
    v = _bdot(u, wv_ref[...])

    gates = small_ref[...] + bias_ref[...]
    log_f = _log_sigmoid(gates)
    tri = _tri(C, rev)
    cum = _cumsum01(tri.astype(bf16), log_f)
    gates_t = gates.T
    cum_t = cum.T
    d = 1 if rev else 0
    for h in range(HEADS):
        sl = slice(h * DH, (h + 1) * DH)
        li = LANE_ML_I + 4 * d + h
        lf = LANE_ML_F + 4 * d + h
        b_col = cum[:, lf:lf + 1]
        b_row = cum_t[lf:lf + 1, :]
        i_col = gates[:, li:li + 1]
        i_row = gates_t[li:li + 1, :]
        b_end = b_col[0:1, :] if rev else b_col[C - 1:C, :]
        m_prev = m_ref[h][0:1, 0:1]
        s_prev = s_ref[h]
        n_prev = n_ref[h][0:1, :]
        qh = q[:, sl]
        kh = k[:, sl]
        vh = v[:, sl]
        dmat = jnp.where(tri, b_col - b_row + i_row, NEG)
        m_in = b_col + m_prev
        m_t = jnp.maximum(m_in, jnp.max(dmat, axis=1, keepdims=True))
        p = jnp.exp(dmat - m_t) * _bdot_nt(qh, kh)
        a_in = jnp.exp(m_in - m_t)
        num = _bdot(p, vh) + a_in * _bdot(qh, s_prev)
        den = jnp.sum(p, axis=1, keepdims=True) + a_in * jnp.sum(qh * n_prev, axis=1, keepdims=True)
        hd = num / jnp.maximum(jnp.abs(den), jnp.exp(-m_t))
        g_col = b_end - b_col + i_col
        m_loc = jnp.max(g_col, axis=0, keepdims=True)
        kw = kh * jnp.exp(g_col - m_loc)
        m_new = jnp.maximum(b_end + m_prev, m_loc)
        a_old = jnp.exp(b_end + m_prev - m_new)
        a_new = jnp.exp(m_loc - m_new)
        s_ref[h] = a_old * s_prev + a_new * _bdot_tn(kw, vh)
        n_ref[h] = jnp.broadcast_to(a_old * n_prev + a_new * jnp.sum(kw, axis=0, keepdims=True), (8, DH))
        m_ref[h] = jnp.broadcast_to(m_new, (8, DH))
        if final:
            hh = _sigmoid(op_ref[:, sl].astype(f32)) * (hd + hb_ref[:, sl])
            hh = _head_norm(hh, center=True) * ng_ref[:, sl] + skip_ref[:, sl] * xc[:, sl]
            o_ref[:, sl] = (hh * _silu(zg_ref[:, sl].astype(f32))).astype(o_ref.dtype)
        else:
            o_ref[:, sl] = hd


def _ml_branch(proj, small, conv_w, conv_b, wq, wk, wv, bias_small, norm_gain, skip, B, L):
    T = B * L
    nc = L // CHUNK
    per = CHUNK // HALO
    last = T // HALO - 1

    def call(rev, final, extra):
        row = _chunk_row(nc, rev)
        prev_spec = pl.BlockSpec((HALO, BR), lambda b, c: (jnp.maximum(row(b, c) * per - 1, 0), BLK_ML))
        next_spec = pl.BlockSpec((HALO, BR), lambda b, c: (jnp.minimum((row(b, c) + 1) * per, last), BLK_ML))
        in_specs = [_tok_spec(nc, rev, BLK_ML), prev_spec, next_spec, _tok_spec(nc, rev, 0, N_SMALL),
                    _const_spec((3, BR)), _const_spec((1, BR)),
                    _const_spec((BR, BR)), _const_spec((BR, BR)), _const_spec((BR, BR)),
                    _const_spec((1, 128))]
        args = [proj, proj, proj, small, conv_w, conv_b, wq, wk, wv, bias_small]
        if final:
            in_specs += [_tok_spec(nc, rev, BLK_ML + 1), _tok_spec(nc, rev, BLK_ML + 2), _tok_spec(nc, rev, 0),
                         _const_spec((1, BR)), _const_spec((1, BR))]
            args += [proj, proj, extra, norm_gain, skip]
        return pl.pallas_call(
            functools.partial(_ml_kernel, rev, final),
            grid=(B, nc),
            in_specs=in_specs,
            out_specs=_tok_spec(nc, rev, 0),
            out_shape=jax.ShapeDtypeStruct((T, BR), bf16 if final else f32),
            scratch_shapes=[pltpu.VMEM((HEADS, DH, DH), f32), pltpu.VMEM((HEADS, 8, DH), f32),
                            pltpu.VMEM((HEADS, 8, DH), f32)],
            compiler_params=_params(("arbitrary", "arbitrary")),
            name="ml_fwd" if final else "ml_bwd",
        )(*args)

    hb = call(True, False, None)
    return call(False, True, hb)


def _hy_filt_kernel(emb_ref, w1_ref, b1_ref, w2_ref, b2_ref, fr_ref, w3_ref, dl_ref, taps_ref, sc_ref):
    r = pl.program_id(0)

    @pl.when(r == 0)
    def _():
        sc_ref[...] = jnp.zeros_like(sc_ref)

    emb = emb_ref[...]
    fr = fr_ref[...]
    hmid = jnp.sin(fr * (_dot3(emb, w1_ref[...]) + b1_ref[...]))
    hmid = jnp.sin(fr * (_dot3(hmid, w2_ref[...]) + b2_ref[...]))
    hout = _dot3(hmid, w3_ref[0])
    t = emb[:, 0:1]
    valid = emb[:, HY_EMB:HY_EMB + 1]
    taps = hout * jnp.exp(-t * dl_ref[...]) * valid
    taps_ref[...] = taps.astype(taps_ref.dtype)
    sc_ref[...] += jnp.sum(taps * taps, axis=0, keepdims=True)

    @pl.when(r == pl.num_programs(0) - 1)
    def _():
        sc_ref[...] = lax.rsqrt(sc_ref[...] + EPS)


def _hy_filter(emb, w1p, b1, w2, b2, freq, w3d, deltas2, L):
    n = 2 * L
    tr = 512
    half = L // tr
    return pl.pallas_call(
        _hy_filt_kernel,
        grid=(n // tr,),
        in_specs=[
            pl.BlockSpec((tr, HY_EMB_PAD), lambda r: (r, 0)),
            pl.BlockSpec((HY_EMB_PAD, HY_HIDDEN), lambda r: (0, 0)),
            pl.BlockSpec((1, HY_HIDDEN), lambda r: (0, 0)),
            pl.BlockSpec((HY_HIDDEN, HY_HIDDEN), lambda r: (0, 0)),
            pl.BlockSpec((1, HY_HIDDEN), lambda r: (0, 0)),
            pl.BlockSpec((1, HY_HIDDEN), lambda r: (0, 0)),
            pl.BlockSpec((1, HY_HIDDEN, 2 * BR), lambda r: (r // half, 0, 0)),
            pl.BlockSpec((1, 2 * BR), lambda r: (0, 0)),
        ],
        out_specs=[pl.BlockSpec((tr, 2 * BR), lambda r: (r, 0)),
                   pl.BlockSpec((1, 2 * BR), lambda r: (0, 0))],
        out_shape=[jax.ShapeDtypeStruct((n, 2 * BR), bf16), jax.ShapeDtypeStruct((1, 2 * BR), f32)],
        compiler_params=_params(("arbitrary",)),
        name="hy_filter",
    )(emb, w1p, b1, w2, b2, freq, w3d, deltas2)


def _seq_chunk(ref, i, nchunks, w, b):
    r0 = pl.multiple_of(i * FFT_N2, FFT_N2)
    x = ref[pl.ds(r0, FFT_N2), :].astype(f32)
    if w is None:
        return x
    total = nchunks * FFT_N2
    p0 = pl.multiple_of(jnp.maximum(r0 - HALO, 0), HALO)
    n0 = pl.multiple_of(jnp.minimum(r0 + FFT_N2, total - HALO), HALO)
    prev = ref[pl.ds(p0, HALO), :].astype(f32)
    nxt = ref[pl.ds(n0, HALO), :].astype(f32)
    return _conv3(x, prev, nxt, i > 0, i < nchunks - 1, w, b)


def _dft_outer_kernel(K, k1t, conv, *refs):
    if conv:
        x_ref, cw_ref, cb_ref, f_ref, o_ref, xs_ref = refs
        w, b = cw_ref[0], cb_ref[0]
    else:
        x_ref, f_ref, o_ref, xs_ref = refs
        w = b = None

    @pl.when(pl.program_id(2) == 0)
    def _():
        def fill(i, carry):
            xs_ref[pl.ds(pl.multiple_of(i * PITCH_X, 8), FFT_N2), :] = _seq_chunk(x_ref, i, K, w, b)
            return carry
        lax.fori_loop(0, K, fill, 0)

    f = f_ref[0]

    def body(g, carry):
        j0 = g * FFT_NJ
        x = jnp.concatenate([xs_ref[pl.ds(j0 + i, K, stride=PITCH_X), :] for i in range(FFT_NJ)], axis=1)
        r = jnp.dot(f, x.astype(bf16), preferred_element_type=f32)
        for i in range(FFT_NJ):
            sl = slice(i * 128, (i + 1) * 128)
            o_ref[pl.ds(j0 + i, k1t, stride=PITCH_A), :] = r[:k1t, sl]
            o_ref[pl.ds(FFT_N2 + j0 + i, k1t, stride=PITCH_A), :] = r[k1t:, sl]
        return carry

    lax.fori_loop(0, FFT_N2 // FFT_NJ, body, 0)
    zero = jnp.zeros((k1t, 128), f32)
    for i in range(PITCH_A - 2 * FFT_N2):
        o_ref[pl.ds(2 * FFT_N2 + i, k1t, stride=PITCH_A), :] = zero


def _dft_outer(x, col0, ncol, fmat, B, K, K1p, k1t, conv=None):
    nt = K1p // k1t
    in_specs = [pl.BlockSpec((K * FFT_N2, 128), lambda b, c, t: (b, col0 + c))]
    args = [x]
    if conv is not None:
        cw, cb, s = conv
        in_specs += [pl.BlockSpec((1, 3, 128), lambda b, c, t: (s, 0, c)),
                     pl.BlockSpec((1, 1, 128), lambda b, c, t: (s, 0, c))]
        args += [cw, cb]
    in_specs.append(pl.BlockSpec((1, 2 * k1t, K), lambda b, c, t: (t, 0, 0)))
    args.append(fmat)
    return pl.pallas_call(
        functools.partial(_dft_outer_kernel, K, k1t, conv is not None),
        grid=(B, ncol, nt),
        in_specs=in_specs,
        out_specs=pl.BlockSpec((k1t * PITCH_A, 128), lambda b, c, t: (b * nt + t, c)),
        out_shape=jax.ShapeDtypeStruct((B * K1p * PITCH_A, 128 * ncol), f32),
        scratch_shapes=[pltpu.VMEM((K * PITCH_X, 128), f32)],
        compiler_params=_params(("arbitrary", "arbitrary", "arbitrary")),
        name="dft_outer",
    )(*args)


def _dft_inner_kernel(conv, *refs):
    if conv:
        a_ref, gf_ref, h_ref, gi_ref, o_ref = refs
    else:
        a_ref, gf_ref, o_ref = refs
    n2 = FFT_N2
    dot = functools.partial(jnp.dot, preferred_element_type=f32)
    for kk in range(FFT_K1T):
        gf = gf_ref[kk]
        x = dot(gf[:, :n2], a_ref[0, kk, :n2].astype(bf16)) + dot(gf[:, n2:], a_ref[0, kk, n2:2 * n2].astype(bf16))
        if conv:
            xr, xi = x[:n2], x[n2:]
            hr, hi = h_ref[0, kk, :n2], h_ref[0, kk, n2:2 * n2]
            yr = (xr * hr - xi * hi).astype(bf16)
            yi = (xr * hi + xi * hr).astype(bf16)
            gi = gi_ref[kk]
            x = dot(gi[:, :n2], yr) + dot(gi[:, n2:], yi)
        o_ref[0, kk, :2 * n2] = x
        o_ref[0, kk, 2 * n2:] = jnp.zeros((PITCH_A - 2 * n2, BR), f32)


def _dft_inner(a4, gf, h4=None, gi=None, hblk=0):
    B, k1p, pa, W = a4.shape
    conv = h4 is not None
    kt = FFT_K1T
    r2 = 2 * FFT_N2
    in_specs = [pl.BlockSpec((1, kt, pa, BR), lambda k, w, b: (b, k, 0, w)),
                pl.BlockSpec((kt, r2, r2), lambda k, w, b: (k, 0, 0))]
    args = [a4, gf]
    if conv:
        in_specs += [pl.BlockSpec((1, kt, pa, BR), lambda k, w, b: (0, k, 0, hblk)),
                     pl.BlockSpec((kt, r2, r2), lambda k, w, b: (k, 0, 0))]
        args += [h4, gi]
    return pl.pallas_call(
        functools.partial(_dft_inner_kernel, conv),
        grid=(k1p // kt, W // BR, B),
        in_specs=in_specs,
        out_specs=pl.BlockSpec((1, kt, pa, BR), lambda k, w, b: (b, k, 0, w)),
        out_shape=jax.ShapeDtypeStruct((B, k1p, pa, W), f32),
        compiler_params=_params(("arbitrary", "arbitrary", "arbitrary")),
        name="dft_inner_conv" if conv else "dft_inner_spec",
    )(*args)


def _idft_outer_kernel(kh, K1p, final, conv_s, *refs):
    it = iter(refs)
    b_ref, fr_ref, fi_ref, mul_ref, s_ref, sc_ref, sk_ref, cwm_ref, cbm_ref = [next(it) for _ in range(9)]
    cws_ref, cbs_ref = (next(it), next(it)) if conv_s else (None, None)
    g_ref = next(it) if final else None
    o_ref, ys_ref = next(it), next(it)
    fr = fr_ref[...]
    fi = fi_ref[...]
    dot = functools.partial(jnp.dot, preferred_element_type=f32)

    def body(g, carry):
        j0 = g * FFT_NJ
        br = jnp.concatenate([b_ref[pl.ds(j0 + i, K1p, stride=PITCH_A), :] for i in range(FFT_NJ)], axis=1)
        bi = jnp.concatenate([b_ref[pl.ds(FFT_N2 + j0 + i, K1p, stride=PITCH_A), :] for i in range(FFT_NJ)], axis=1)
        y = dot(fr, br.astype(bf16)) + dot(fi, bi.astype(bf16))
        for i in range(FFT_NJ):
            ys_ref[pl.ds(j0 + i, kh, stride=PITCH_X), :] = y[:, i * 128:(i + 1) * 128]
        return carry

    lax.fori_loop(0, FFT_N2 // FFT_NJ, body, 0)
    sc = sc_ref[...]
    sk = sk_ref[...]

    def epi(i, carry):
        y = ys_ref[pl.ds(pl.multiple_of(i * PITCH_X, 8), FFT_N2), :]
        m = _seq_chunk(mul_ref, i, kh, cwm_ref[0], cbm_ref[0])
        s = _seq_chunk(s_ref, i, kh, cws_ref[0], cbs_ref[0]) if conv_s else _seq_chunk(s_ref, i, kh, None, None)
        z = m * (y * sc + s * sk)
        if final:
            z = z * _silu(_seq_chunk(g_ref, i, kh, None, None))
        o_ref[pl.ds(pl.multiple_of(i * FFT_N2, FFT_N2), FFT_N2), :] = z.astype(o_ref.dtype)
        return carry

    lax.fori_loop(0, kh, epi, 0)


def _idft_outer(b2, fr, fi, mul, s, scale, skip, conv_w, conv_b, B, kh, K1p, g=None):
    L = kh * FFT_N2
    once = pl.Buffered(1)
    mul_a, mul_c, mul_s = mul
    s_a, s_c, s_s = s
    conv_s = s_s is not None
    vec = pl.BlockSpec((1, 128), lambda b, c: (0, c))
    in_specs = [pl.BlockSpec((K1p * PITCH_A, 128), lambda b, c: (b, c), pipeline_mode=once),
                pl.BlockSpec((kh, K1p), lambda b, c: (0, 0)), pl.BlockSpec((kh, K1p), lambda b, c: (0, 0)),
                pl.BlockSpec((L, 128), lambda b, c: (b, mul_c + c), pipeline_mode=once),
                pl.BlockSpec((L, 128), lambda b, c: (b, s_c + c), pipeline_mode=once),
                vec, vec,
                pl.BlockSpec((1, 3, 128), lambda b, c: (mul_s, 0, c)),
                pl.BlockSpec((1, 1, 128), lambda b, c: (mul_s, 0, c))]
    args = [b2, fr, fi, mul_a, s_a, scale, skip, conv_w, conv_b]
    if conv_s:
        in_specs += [pl.BlockSpec((1, 3, 128), lambda b, c: (s_s, 0, c)),
                     pl.BlockSpec((1, 1, 128), lambda b, c: (s_s, 0, c))]
        args += [conv_w, conv_b]
    if g is not None:
        g_a, g_c = g
        in_specs.append(pl.BlockSpec((L, 128), lambda b, c: (b, g_c + c), pipeline_mode=once))
        args.append(g_a)
    return pl.pallas_call(
        functools.partial(_idft_outer_kernel, kh, K1p, g is not None, conv_s),
        grid=(B, BR // 128),
        in_specs=in_specs,
        out_specs=pl.BlockSpec((L, 128), lambda b, c: (b, c)),
        out_shape=jax.ShapeDtypeStruct((B * L, BR), bf16),
        scratch_shapes=[pltpu.VMEM((kh * PITCH_X, 128), f32)],
        compiler_params=_params(("arbitrary", "arbitrary")),
        name="idft_outer",
    )(*args)


def _fft_tables(L):
    n = 2 * L
    n2 = FFT_N2
    n1 = n // n2
    kh = n1 // 2
    k1p = -(-(kh + 1) // 16) * 16
    k1t = k1p if k1p <= 48 else 48
    nt = k1p // k1t
    two_pi = 2.0 * math.pi
    k1 = jnp.arange(k1p, dtype=jnp.int32)

    def outer(K):
        m = jnp.arange(K, dtype=jnp.int32)
        ang = (two_pi / n1) * ((k1[:, None] * m[None, :]) % n1).astype(f32)
        re = jnp.cos(ang).reshape(nt, k1t, K)
        im = (-jnp.sin(ang)).reshape(nt, k1t, K)
        return jnp.concatenate([re, im], axis=1).astype(bf16)

    r1 = jnp.arange(kh, dtype=jnp.int32)
    wgt = jnp.where((k1 == 0) | (k1 == kh), 1.0, jnp.where(k1 < kh, 2.0, 0.0)).astype(f32)
    angi = (two_pi / n1) * ((r1[:, None] * k1[None, :]) % n1).astype(f32)
    fi_re = (jnp.cos(angi) * wgt[None, :]).astype(bf16)
    fi_im = (-jnp.sin(angi) * wgt[None, :]).astype(bf16)
    kk = k1[:, None, None]
    a = jnp.arange(n2, dtype=jnp.int32)[None, :, None]
    c = jnp.arange(n2, dtype=jnp.int32)[None, None, :]
    angp = (two_pi / n) * ((a * c * n1 + c * kk) % n).astype(f32)
    mr, mi = jnp.cos(angp), -jnp.sin(angp)
    g_fwd = jnp.concatenate([jnp.concatenate([mr, -mi], axis=2),
                             jnp.concatenate([mi, mr], axis=2)], axis=1).astype(bf16)
    angq = (two_pi / n) * ((a * c * n1 + a * kk) % n).astype(f32)
    vr, vi = jnp.cos(angq), jnp.sin(angq)
    g_inv = jnp.concatenate([jnp.concatenate([vr, -vi], axis=2),
                             jnp.concatenate([vi, vr], axis=2)], axis=1).astype(bf16)
    return dict(n=n, n1=n1, kh=kh, k1p=k1p, k1t=k1t, f_half=outer(kh), f_full=outer(n1),
                fi_re=fi_re, fi_im=fi_im, g_fwd=g_fwd, g_inv=g_inv)


def _hy_embedding(L):
    n = jnp.arange(2 * L, dtype=jnp.int32)
    pos = jnp.where(n < L, n, 2 * L - n).astype(f32)
    t = pos / (L - 1)
    f = jnp.linspace(1e-4, HY_BANDS - 1, HY_BANDS, dtype=f32)
    ang = (2.0 * math.pi / L) * pos[:, None] * f[None, :]
    valid = (n != L).astype(f32)
    pad = jnp.zeros((2 * L, HY_EMB_PAD - HY_EMB - 1), f32)
    return jnp.concatenate([t[:, None], jnp.cos(ang), -jnp.sin(ang), valid[:, None], pad], axis=-1)


def _hy_branch(proj, lw, emb, tabs, B, L):
    n1, kh, k1p, k1t = tabs["n1"], tabs["kh"], tabs["k1p"], tabs["k1t"]
    cw, cb = lw["hy_conv_w"], lw["hy_conv_b"]
    taps, scale = _hy_filter(emb, lw["hy_w1"], lw["hy_b1"], lw["hy_w2"], lw["hy_b2"], lw["hy_freq"],
                             lw["hy_w3"], lw["hy_deltas"], L)
    ta = _dft_outer(taps, 0, 2 * BR // 128, tabs["f_full"], 1, n1, k1p, k1t)
    h4 = _dft_inner(ta.reshape(1, k1p, PITCH_A, 2 * BR), tabs["g_fwd"])
    scale = scale * (1.0 / tabs["n"])
    col = lambda s: (BLK_HY + s) * (BR // 128)

    def conv(x, col0, order, stream):
        a = _dft_outer(x, col0, BR // 128, tabs["f_half"], B, kh, k1p, k1t,
                       None if stream is None else (cw, cb, stream))
        b4 = _dft_inner(a.reshape(B, k1p, PITCH_A, BR), tabs["g_fwd"], h4, tabs["g_inv"], order)
        return b4.reshape(B * k1p * PITCH_A, BR)

    def inv(b2, order, mul, s, g=None):
        return _idft_outer(b2, tabs["fi_re"], tabs["fi_im"], mul, s, scale[:, order * BR:(order + 1) * BR],
                           lw["hy_skip"][order:order + 1], cw, cb, B, kh, k1p, g)

    z = inv(conv(proj, col(0), 0, 0), 0, (proj, col(1), 1), (proj, col(0), 0))
    return inv(conv(z, 0, 1, None), 1, (proj, col(2), 2), (z, 0, None), (proj, col(3)))


def _rotary_tables(L):
    inv = ROPE_BASE ** (-jnp.arange(0, DH, 2, dtype=f32) / DH)
    ang = jnp.arange(L, dtype=f32)[:, None] * inv[None, :]
    cos, sin = jnp.cos(ang), jnp.sin(ang)
    return jnp.concatenate([cos, cos], axis=-1), jnp.concatenate([-sin, sin], axis=-1)


def _block_diag(w):
    n, e, fo = w.shape
    eye = jnp.eye(n, dtype=w.dtype)
    return (eye[:, None, :, None] * w[:, :, None, :]).reshape(n * e, n * fo)


def _permute_w_in(w_in):
    ref_gla_small = 2048 + 1536
    ref_hy = ref_gla_small + 2 * GLA_RANK
    ref_ml = ref_hy + 2048
    ref_ml_small = ref_ml + 1536
    main = jnp.concatenate([w_in[..., :ref_gla_small], w_in[..., ref_hy:ref_ml], w_in[..., ref_ml:ref_ml_small]],
                           axis=-1)
    small = jnp.concatenate([w_in[..., ref_gla_small:ref_hy], w_in[..., ref_ml_small:]], axis=-1)
    small = jnp.pad(small, [(0, 0)] * (small.ndim - 1) + [(0, N_SMALL - small.shape[-1])])
    return main.astype(bf16), small.astype(bf16)


def _layer_weights(l, p):
    lw = {}
    lw["gla_wa"] = jnp.stack([
        jnp.pad(p["gla_w_a"][l, d], ((LANE_GLA_A + GLA_RANK * d, 128 - LANE_GLA_A - GLA_RANK * (d + 1)), (0, 0)))
        for d in range(2)]).astype(bf16)
    lw["gla_ba"] = p["gla_b_a"][l]
    lw["gla_ng"] = p["gla_norm_gain"][l].reshape(1, DH)
    lw["hy_conv_w"] = p["hy_conv_w"][l].reshape(3, 3, BR).transpose(1, 0, 2)
    lw["hy_conv_b"] = p["hy_conv_b"][l].reshape(3, 1, BR)
    lw["hy_w1"] = jnp.pad(p["hy_w1"][l], ((0, HY_EMB_PAD - HY_EMB), (0, 0)))
    lw["hy_b1"] = p["hy_b1"][l].reshape(1, HY_HIDDEN)
    lw["hy_w2"] = p["hy_w2"][l]
    lw["hy_b2"] = p["hy_b2"][l].reshape(1, HY_HIDDEN)
    lw["hy_freq"] = p["hy_freq"][l].reshape(1, HY_HIDDEN)
    lw["hy_w3"] = p["hy_w3"][l].reshape(HY_HIDDEN, 2, 2, BR).transpose(2, 0, 1, 3).reshape(2, HY_HIDDEN, 2 * BR)
    deltas = jnp.abs(jnp.linspace(HY_MIN_DECAY, HY_MAX_DECAY, BR, dtype=f32))
    lw["hy_deltas"] = jnp.concatenate([deltas, deltas]).reshape(1, 2 * BR)
    lw["hy_skip"] = p["hy_skip"][l]
    lw["ml_conv_w"] = p["ml_conv_w"][l]
    lw["ml_conv_b"] = p["ml_conv_b"][l].reshape(1, BR)
    lw["ml_wq"] = _block_diag(p["ml_wq"][l]).astype(bf16)
    lw["ml_wk"] = _block_diag(p["ml_wk"][l]).astype(bf16)
    lw["ml_wv"] = _block_diag(p["ml_wv"][l]).astype(bf16)
    bias = jnp.concatenate([p["ml_b_i"][l].reshape(-1), p["ml_b_f"][l].reshape(-1)])
    lw["ml_bias"] = jnp.pad(bias, (LANE_ML_I, 128 - LANE_ML_I - bias.shape[0])).reshape(1, 128)
    lw["ml_ng"] = p["ml_norm_gain"][l].reshape(1, BR)
    lw["ml_skip"] = p["ml_skip"][l].reshape(1, BR)
    return lw


def _mix(proj, small, lw, consts, B, L):
    cosf, sins, emb, tabs = consts
    return (
        _ret_branch(proj, cosf, sins, B, L),
        _gla_branch(proj, small, lw["gla_wa"], lw["gla_ba"], lw["gla_ng"], B, L),
        _hy_branch(proj, lw, emb, tabs, B, L),
        _ml_branch(proj, small, lw["ml_conv_w"], lw["ml_conv_b"], lw["ml_wq"], lw["ml_wk"], lw["ml_wv"],
                   lw["ml_bias"], lw["ml_ng"], lw["ml_skip"], B, L),
    )


def kernel(x_prompt, x_sample, c_prompt, c_sample, norm_gain, w_ada, b_ada, w_in, gla_w_a, gla_b_a, gla_norm_gain, hy_conv_w, hy_conv_b, hy_w1, hy_b1, hy_w2, hy_b2, hy_freq, hy_w3, hy_skip, ml_conv_w, ml_conv_b, ml_wq, ml_wk, ml_wv, ml_b_i, ml_b_f, ml_norm_gain, ml_skip, w_branch, w_gate, b_gate, w_out, final_gain):
    p = dict(gla_w_a=gla_w_a, gla_b_a=gla_b_a, gla_norm_gain=gla_norm_gain, hy_conv_w=hy_conv_w,
             hy_conv_b=hy_conv_b, hy_w1=hy_w1, hy_b1=hy_b1, hy_w2=hy_w2, hy_b2=hy_b2, hy_freq=hy_freq,
             hy_w3=hy_w3, hy_skip=hy_skip, ml_conv_w=ml_conv_w, ml_conv_b=ml_conv_b, ml_wq=ml_wq,
             ml_wk=ml_wk, ml_wv=ml_wv, ml_b_i=ml_b_i, ml_b_f=ml_b_f, ml_norm_gain=ml_norm_gain,
             ml_skip=ml_skip)
    groups = []
    for x, c in ((x_prompt, c_prompt), (x_sample, c_sample)):
        B, L, _ = x.shape
        groups.append(dict(B=B, L=L, x=x.reshape(B * L, D_MODEL),
                           consts=_rotary_tables(L) + (_hy_embedding(L), _fft_tables(L))))
    nb = [g["B"] for g in groups]
    rows = -(-sum(nb) // 8) * 8
    c_all = jnp.concatenate([c_prompt, c_sample, jnp.zeros((rows - sum(nb), D_MODEL), f32)], axis=0)
    mod_all = _ada_all(c_all, w_ada, b_ada)

    w_in_p, w_in_s = _permute_w_in(w_in)
    wg = w_gate.astype(bf16)
    wb = w_branch.astype(bf16)
    wo = w_out.astype(bf16)
    fg = final_gain.reshape(1, D_MODEL)
    for l in range(DEPTH):
        lw = _layer_weights(l, p)
        gain = norm_gain[l].reshape(1, D_MODEL)
        bg = b_gate[l].reshape(4, 1, D_MODEL)
        start = 0
        for g in groups:
            B, L = g["B"], g["L"]
            mod = mod_all[l, start:start + B].reshape(B, 3, D_MODEL)
            start += B
            proj, small = _inproj(g["x"], mod, gain, w_in_p[l], w_in_s[l], L)
            branches = _mix(proj, small, lw, g["consts"], B, L)
            g["x"] = _merge(g["x"], mod, gain, branches, wg[l], bg, wb[l], wo[l], fg, L, l == DEPTH - 1)
    return tuple(g["x"].reshape(g["B"], g["L"], D_MODEL) for g in groups)
```

```python
import functools
import math

import jax
import jax.numpy as jnp
from jax import lax
from jax.experimental import pallas as pl
from jax.experimental.pallas import tpu as pltpu

f32 = jnp.float32
bf16 = jnp.bfloat16

D_MODEL = 2048
DEPTH = 4
BR = 512
HEADS = 4
DH = 128
ROPE_BASE = 10000.0
GLA_DK = 64
GLA_KW = 256
GLA_RANK = 16
GLA_GATE_NORM = 16.0
HY_EMB = 33
HY_EMB_PAD = 40
HY_BANDS = 16
HY_HIDDEN = 64
HY_MIN_DECAY = math.log(1e-2) / 1.5
HY_MAX_DECAY = math.log(1e-2) / 0.3
ML_BLOCK = 4
EPS = 1e-6
NEG = -1e30

BLK_RET = 0
BLK_GLA_QK = 4
BLK_GLA_V = 5
BLK_GLA_G = 6
BLK_HY = 7
BLK_ML = 11
N_PROJ = 7168
N_SMALL = 128
LANE_GLA_A = 0
LANE_ML_I = 32
LANE_ML_F = 40

CHUNK = 128
HALO = 16
FFT_N2 = 128
FFT_K1T = 8
FFT_NJ = 4
PITCH_X = FFT_N2 + 8
PITCH_A = 2 * FFT_N2 + 8
TM_DENSE = 1024
TN_IN = 1792
TM_MERGE = 512
TN_MERGE = 512
SEQ_PER_STEP = 2
VMEM_LIMIT = 60 * 1024 * 1024

LOG_GAMMA = tuple(math.log(1.0 - 2.0 ** (-5.0 - h)) for h in range(HEADS))


def _params(sem):
    return pltpu.CompilerParams(dimension_semantics=sem, vmem_limit_bytes=VMEM_LIMIT)


def _bdot(a, b):
    return jnp.dot(a.astype(bf16), b.astype(bf16), preferred_element_type=f32)


def _bdot_nt(a, b):
    return lax.dot_general(a.astype(bf16), b.astype(bf16), (((1,), (1,)), ((), ())),
                           preferred_element_type=f32)


def _bdot_tn(a, b):
    return jnp.dot(a.T.astype(bf16), b.astype(bf16), preferred_element_type=f32)


def _dot3(a, b):
    a1 = a.astype(bf16)
    ra = a - a1.astype(f32)
    a2 = ra.astype(bf16)
    a3 = (ra - a2.astype(f32)).astype(bf16)
    b1 = b.astype(bf16)
    rb = b - b1.astype(f32)
    b2 = rb.astype(bf16)
    b3 = (rb - b2.astype(f32)).astype(bf16)
    d = functools.partial(jnp.dot, preferred_element_type=f32)
    return (d(a1, b1) + (d(a1, b2) + d(a2, b1))) + ((d(a2, b2) + d(a1, b3)) + d(a3, b1))


def _cumsum01(tri, x):
    x1 = x.astype(bf16)
    r = x - x1.astype(f32)
    x2 = r.astype(bf16)
    x3 = (r - x2.astype(f32)).astype(bf16)
    d = functools.partial(jnp.dot, preferred_element_type=f32)
    return d(tri, x1) + (d(tri, x2) + d(tri, x3))


def _sigmoid(x):
    return jax.nn.sigmoid(x)


def _silu(x):
    return x * jax.nn.sigmoid(x)


def _log_sigmoid(x):
    return jnp.minimum(x, 0.0) - jnp.log1p(jnp.exp(-jnp.abs(x)))


def _tri(n, rev):
    ii = lax.broadcasted_iota(jnp.int32, (n, n), 0)
    jj = lax.broadcasted_iota(jnp.int32, (n, n), 1)
    return (jj >= ii) if rev else (jj <= ii)


def _prev_rows(x, first):
    row = lax.broadcasted_iota(jnp.int32, x.shape, 0)
    return jnp.where(row == 0, first, pltpu.roll(x, 1, 0))


def _next_rows(x, last):
    n = x.shape[0]
    row = lax.broadcasted_iota(jnp.int32, x.shape, 0)
    return jnp.where(row == n - 1, last, pltpu.roll(x, n - 1, 0))


def _conv3(x, prev, nxt, has_prev, has_next, w, b):
    first = jnp.where(has_prev, prev[HALO - 1:HALO, :], 0.0)
    last = jnp.where(has_next, nxt[0:1, :], 0.0)
    return _prev_rows(x, first) * w[0:1] + x * w[1:2] + _next_rows(x, last) * w[2:3] + b


def _head_norm(x, center):
    if center:
        x = x - jnp.mean(x, axis=-1, keepdims=True)
    return x * lax.rsqrt(jnp.mean(x * x, axis=-1, keepdims=True) + EPS)


def _ada_kernel(c_ref, w_ref, b_ref, o_ref):
    o_ref[0] = _bdot(_silu(c_ref[...]), w_ref[0]) + b_ref[0]


def _ada_all(c_all, w_ada, b_ada):
    rows = c_all.shape[0]
    tn = 1536
    return pl.pallas_call(
        _ada_kernel,
        grid=(DEPTH, 3 * D_MODEL // tn),
        in_specs=[
            pl.BlockSpec((rows, D_MODEL), lambda l, j: (0, 0)),
            pl.BlockSpec((1, D_MODEL, tn), lambda l, j: (l, 0, j)),
            pl.BlockSpec((1, 1, tn), lambda l, j: (l, 0, j)),
        ],
        out_specs=pl.BlockSpec((1, rows, tn), lambda l, j: (l, 0, j)),
        out_shape=jax.ShapeDtypeStruct((DEPTH, rows, 3 * D_MODEL), f32),
        compiler_params=_params(("arbitrary", "arbitrary")),
        name="ada",
    )(c_all, w_ada, b_ada.reshape(DEPTH, 1, 3 * D_MODEL))


ROWS_NORM = 256


def _modulated_norm(x, mod, gain):
    y = x * lax.rsqrt(jnp.mean(x * x, axis=-1, keepdims=True) + EPS) * gain
    return y * (1.0 + mod[1:2]) + mod[0:1]


def _norm_to_scratch(x_ref, mod_ref, gain_ref, h_ref):
    mod = mod_ref[0]
    gain = gain_ref[...]

    def step(i, carry):
        rows = pl.ds(pl.multiple_of(i * ROWS_NORM, ROWS_NORM), ROWS_NORM)
        h_ref[rows, :] = _modulated_norm(x_ref[rows, :], mod, gain).astype(bf16)
        return carry

    lax.fori_loop(0, x_ref.shape[0] // ROWS_NORM, step, 0)


def _inproj_kernel(x_ref, mod_ref, gain_ref, w_ref, ws_ref, o_ref, os_ref, h_ref):
    @pl.when(pl.program_id(1) == 0)
    def _():
        _norm_to_scratch(x_ref, mod_ref, gain_ref, h_ref)
        os_ref[...] = jnp.dot(h_ref[...], ws_ref[...], preferred_element_type=f32)

    o_ref[...] = jnp.dot(h_ref[...], w_ref[...], preferred_element_type=f32).astype(bf16)


def _inproj(x, mod, gain, w, ws, L):
    T = x.shape[0]
    tm = min(TM_DENSE, L)
    return pl.pallas_call(
        _inproj_kernel,
        grid=(T // tm, N_PROJ // TN_IN),
        in_specs=[
            pl.BlockSpec((tm, D_MODEL), lambda i, j: (i, 0)),
            pl.BlockSpec((1, 3, D_MODEL), lambda i, j: ((i * tm) // L, 0, 0)),
            pl.BlockSpec((1, D_MODEL), lambda i, j: (0, 0)),
            pl.BlockSpec((D_MODEL, TN_IN), lambda i, j: (0, j)),
            pl.BlockSpec((D_MODEL, N_SMALL), lambda i, j: (0, 0)),
        ],
        out_specs=[pl.BlockSpec((tm, TN_IN), lambda i, j: (i, j)),
                   pl.BlockSpec((tm, N_SMALL), lambda i, j: (i, 0))],
        out_shape=[jax.ShapeDtypeStruct((T, N_PROJ), bf16), jax.ShapeDtypeStruct((T, N_SMALL), f32)],
        scratch_shapes=[pltpu.VMEM((tm, D_MODEL), bf16)],
        compiler_params=_params(("arbitrary", "arbitrary")),
        name="inproj",
    )(x, mod, gain, w, ws)


def _merge_kernel(final, x_ref, mod_ref, gain_ref, b0_ref, b1_ref, b2_ref, b3_ref,
                  wg_ref, bg_ref, wb_ref, wo_ref, fg_ref, o_ref, h_ref, m_ref):
    j = pl.program_id(1)
    tn = bg_ref.shape[-1]
    nj = m_ref.shape[1] // tn

    @pl.when(j == 0)
    def _():
        _norm_to_scratch(x_ref, mod_ref, gain_ref, h_ref)

    h = h_ref[...]
    merged = None
    for bi, br_ref in enumerate((b0_ref, b1_ref, b2_ref, b3_ref)):
        g = _sigmoid(jnp.dot(h, wg_ref[bi], preferred_element_type=f32) + bg_ref[bi])
        t = g * jnp.dot(br_ref[...], wb_ref[bi], preferred_element_type=f32)
        merged = t if merged is None else merged + t
    merged = merged.astype(bf16)
    for jj in range(nj):
        @pl.when(j == jj)
        def _():
            m_ref[:, jj * tn:(jj + 1) * tn] = merged

    @pl.when(j == nj - 1)
    def _():
        o_ref[...] = jnp.dot(m_ref[...], wo_ref[...], preferred_element_type=f32)
        gate = mod_ref[0][2:3]

        def step(i, carry):
            rows = pl.ds(pl.multiple_of(i * ROWS_NORM, ROWS_NORM), ROWS_NORM)
            y = x_ref[rows, :] + gate * o_ref[rows, :]
            if final:
                y = y * lax.rsqrt(jnp.mean(y * y, axis=-1, keepdims=True) + EPS) * fg_ref[...]
            o_ref[rows, :] = y
            return carry

        lax.fori_loop(0, x_ref.shape[0] // ROWS_NORM, step, 0)


def _merge(x, mod, gain, branches, wg, bg, wb, wo, final_gain, L, final):
    T = x.shape[0]
    tm = min(TM_MERGE, L)
    tn = TN_MERGE
    br_spec = pl.BlockSpec((tm, BR), lambda i, j: (i, 0))
    return pl.pallas_call(
        functools.partial(_merge_kernel, final),
        grid=(T // tm, D_MODEL // tn),
        in_specs=[
            pl.BlockSpec((tm, D_MODEL), lambda i, j: (i, 0), pipeline_mode=pl.Buffered(1)),
            pl.BlockSpec((1, 3, D_MODEL), lambda i, j: ((i * tm) // L, 0, 0)),
            pl.BlockSpec((1, D_MODEL), lambda i, j: (0, 0)),
            br_spec, br_spec, br_spec, br_spec,
            pl.BlockSpec((4, D_MODEL, tn), lambda i, j: (0, 0, j)),
            pl.BlockSpec((4, 1, tn), lambda i, j: (0, 0, j)),
            pl.BlockSpec((4, BR, tn), lambda i, j: (0, 0, j)),
            pl.BlockSpec((D_MODEL, D_MODEL), lambda i, j: (0, 0), pipeline_mode=pl.Buffered(1)),
            pl.BlockSpec((1, D_MODEL), lambda i, j: (0, 0)),
        ],
        out_specs=pl.BlockSpec((tm, D_MODEL), lambda i, j: (i, 0)),
        out_shape=jax.ShapeDtypeStruct((T, D_MODEL), f32),
        scratch_shapes=[pltpu.VMEM((tm, D_MODEL), bf16), pltpu.VMEM((tm, D_MODEL), bf16)],
        compiler_params=_params(("arbitrary", "arbitrary")),
        name="merge",
    )(x, mod, gain, *branches, wg, bg, wb, wo, final_gain)


def _chunk_idx(nc, rev):
    if rev:
        return lambda c: nc - 1 - c
    return lambda c: c


def _tok_spec(ns, nc, rev, blk, width=BR):
    idx = _chunk_idx(nc, rev)
    return pl.BlockSpec((ns, CHUNK, width), lambda b, c: (b, idx(c), blk))


def _per_sequence(body, batched, *refs):
    ns = [r for r, f in zip(refs, batched) if f][0].shape[0]
    for s in range(ns):
        body(*[r.at[s] if f else r for r, f in zip(refs, batched)])


def _seqs_per_step(B):
    return SEQ_PER_STEP if B % SEQ_PER_STEP == 0 else 1


def _pos_spec(nc, rev):
    if rev:
        return pl.BlockSpec((CHUNK, DH), lambda b, c: (nc - 1 - c, 0))
    return pl.BlockSpec((CHUNK, DH), lambda b, c: (c, 0))


def _const_spec(shape):
    nd = len(shape)
    return pl.BlockSpec(shape, lambda b, c: (0,) * nd)


def _ret_kernel(rev, final, batched, *refs):
    _per_sequence(functools.partial(_ret_body, rev, final), batched, *refs)


def _ret_body(rev, final, *refs):
    if final:
        q_ref, k_ref, v_ref, cos_ref, sin_ref, g_ref, rb_ref, o_ref, s_ref = refs
    else:
        q_ref, k_ref, v_ref, cos_ref, sin_ref, o_ref, s_ref = refs
    C = CHUNK

    @pl.when(pl.program_id(1) == 0)
    def _():
        s_ref[...] = jnp.zeros_like(s_ref)

    cos = cos_ref[...]
    sin = sin_ref[...]
    ii = lax.broadcasted_iota(jnp.int32, (C, C), 0)
    jj = lax.broadcasted_iota(jnp.int32, (C, C), 1)
    col = lax.broadcasted_iota(jnp.int32, (C, 1), 0).astype(f32)
    dist = (ii - jj).astype(f32)
    for h in range(HEADS):
        lg = LOG_GAMMA[h]
        sl = slice(h * DH, (h + 1) * DH)
        q = q_ref[:, sl].astype(f32)
        k = k_ref[:, sl].astype(f32)
        v = v_ref[:, sl]
        q = q * cos + pltpu.roll(q, DH // 2, 1) * sin
        k = (k * cos + pltpu.roll(k, DH // 2, 1) * sin) * (DH ** -0.5)
        if rev:
            gam = jnp.where(jj > ii, jnp.exp(-dist * lg), 0.0)
            qd = jnp.exp((C - col) * lg)
            kd = jnp.exp(col * lg)
        else:
            gam = jnp.where(jj <= ii, jnp.exp(dist * lg), 0.0)
            qd = jnp.exp((col + 1.0) * lg)
            kd = jnp.exp((C - 1.0 - col) * lg)
        s_prev = s_ref[h]
        o = _bdot(_bdot_nt(q, k) * gam, v) + qd * _bdot(q, s_prev)
        s_ref[h] = math.exp(C * lg) * s_prev + _bdot_tn(k * kd, v)
        if final:
            r = _head_norm(o + rb_ref[:, sl], center=True)
            o_ref[:, sl] = (r * _silu(g_ref[:, sl].astype(f32))).astype(o_ref.dtype)
        else:
            o_ref[:, sl] = o


def _ret_branch(proj, cosf, sins, B, L):
    nc = L // CHUNK
    ns = _seqs_per_step(B)

    def call(rev, final, extra):
        in_specs = [_tok_spec(ns, nc, rev, BLK_RET + i) for i in range(3)]
        in_specs += [_pos_spec(nc, rev), _pos_spec(nc, rev)]
        args = [proj, proj, proj, cosf, sins]
        batched = [True, True, True, False, False]
        if final:
            in_specs += [_tok_spec(ns, nc, rev, BLK_RET + 3), _tok_spec(ns, nc, rev, 0)]
            args += [proj, extra]
            batched += [True, True]
        batched += [True, True]
        return pl.pallas_call(
            functools.partial(_ret_kernel, rev, final, tuple(batched)),
            grid=(B // ns, nc),
            in_specs=in_specs,
            out_specs=_tok_spec(ns, nc, rev, 0),
            out_shape=jax.ShapeDtypeStruct((B, L, BR), bf16 if final else f32),
            scratch_shapes=[pltpu.VMEM((ns, HEADS, DH, DH), f32)],
            compiler_params=_params(("arbitrary", "arbitrary")),
            name="ret_fwd" if final else "ret_bwd",
        )(*args)

    rb = call(True, False, None)
    return call(False, True, rb)


def _gla_kernel(rev, final, batched, *refs):
    _per_sequence(functools.partial(_gla_body, rev, final), batched, *refs)


def _gla_body(rev, final, *refs):
    if final:
        qk_ref, v_ref, small_ref, wa_ref, ba_ref, g_ref, ob_ref, ng_ref, o_ref, st_ref = refs
    else:
        qk_ref, v_ref, small_ref, wa_ref, ba_ref, o_ref, st_ref = refs
    C = CHUNK

    @pl.when(pl.program_id(1) == 0)
    def _():
        st_ref[...] = jnp.zeros_like(st_ref)

    q = qk_ref[:, :GLA_KW].astype(f32) * (GLA_DK ** -0.5)
    k = qk_ref[:, GLA_KW:].astype(f32)
    v = v_ref[...].astype(f32)
    tri = _tri(C, rev)
    log_a = _log_sigmoid(_bdot(small_ref[...], wa_ref[...]) + ba_ref[...]) / GLA_GATE_NORM
    b = _cumsum01(tri.astype(bf16), log_a)
    ref_row = b[C // 2:C // 2 + 1, :]
    b_end = b[0:1, :] if rev else b[C - 1:C, :]
    qt = q * jnp.exp(b - ref_row)
    kt = k * jnp.exp(ref_row - b)
    inter = _bdot_nt(q * jnp.exp(b), st_ref[...])
    lane = lax.broadcasted_iota(jnp.int32, (1, GLA_KW), 1)
    for h in range(HEADS):
        sl = slice(h * DH, (h + 1) * DH)
        a = _bdot_nt(jnp.where(lane // GLA_DK == h, qt, 0.0), kt)
        o = _bdot(jnp.where(tri, a, 0.0), v[:, sl]) + inter[:, sl]
        if final:
            o = _head_norm(o + ob_ref[:, sl], center=False) * ng_ref[...]
            o_ref[:, sl] = (o * _silu(g_ref[:, sl].astype(f32))).astype(o_ref.dtype)
        else:
            o_ref[:, sl] = o
    ee = lax.broadcasted_iota(jnp.int32, (BR, GLA_KW), 0) // DH
    dd = lax.broadcasted_iota(jnp.int32, (BR, GLA_KW), 1) // GLA_DK
    upd = _bdot_tn(v, k * jnp.exp(b_end - b))
    st_ref[...] = st_ref[...] * jnp.exp(b_end) + jnp.where(ee == dd, upd, 0.0)


def _gla_branch(proj, small, wa_pad, ba, norm_gain, B, L):
    nc = L // CHUNK
    ns = _seqs_per_step(B)

    def call(rev, final, extra):
        d = 1 if rev else 0
        in_specs = [_tok_spec(ns, nc, rev, BLK_GLA_QK), _tok_spec(ns, nc, rev, BLK_GLA_V),
                    _tok_spec(ns, nc, rev, 0, N_SMALL),
                    _const_spec((128, GLA_KW)), _const_spec((1, GLA_KW))]
        args = [proj, proj, small, wa_pad[d], ba[d:d + 1]]
        batched = [True, True, True, False, False]
        if final:
            in_specs += [_tok_spec(ns, nc, rev, BLK_GLA_G), _tok_spec(ns, nc, rev, 0), _const_spec((1, DH))]
            args += [proj, extra, norm_gain]
            batched += [True, True, False]
        batched += [True, True]
        return pl.pallas_call(
            functools.partial(_gla_kernel, rev, final, tuple(batched)),
            grid=(B // ns, nc),
            in_specs=in_specs,
            out_specs=_tok_spec(ns, nc, rev, 0),
            out_shape=jax.ShapeDtypeStruct((B, L, BR), bf16 if final else f32),
            scratch_shapes=[pltpu.VMEM((ns, BR, GLA_KW), f32)],
            compiler_params=_params(("arbitrary", "arbitrary")),
            name="gla_fwd" if final else "gla_bwd",
        )(*args)

    ob = call(True, False, None)
    return call(False, True, ob)


def _ml_kernel(rev, final, batched, *refs):
    _per_sequence(functools.partial(_ml_body, rev, final), batched, *refs)


def _ml_body(rev, final, *refs):
    if final:
        (u_ref, up_ref, un_ref, small_ref, cw_ref, cb_ref, wq_ref, wk_ref, wv_ref, bias_ref,
         zg_ref, op_ref, hb_ref, ng_ref, skip_ref, o_ref, s_ref, n_ref, m_ref) = refs
    else:
        (u_ref, up_ref, un_ref, small_ref, cw_ref, cb_ref, wq_ref, wk_ref, wv_ref, bias_ref,
         o_ref, s_ref, n_ref, m_ref) = refs
    C = CHUNK
    c = pl.program_id(1)
    nc = pl.num_programs(1)

    @pl.when(c == 0)
    def _():
        s_ref[...] = jnp.zeros_like(s_ref)
        n_ref[...] = jnp.zeros_like(n_ref)
        m_ref[...] = jnp.full(m_ref.shape, NEG, f32)

    pos = (nc - 1 - c) if rev else c
    u = u_ref[...].astype(f32)
    xc = _silu(_conv3(u, up_ref[...].astype(f32), un_ref[...].astype(f32), pos > 0, pos < nc - 1,
                      cw_ref[...], cb_ref[...]))
    q = _bdot(xc, wq_ref[...])
    k = _bdot(xc, wk_ref[...]) * (DH ** -0.5)
    v = _bdot(u, wv_ref[...])

    gates = small_ref[...] + bias_ref[...]
    log_f = _log_sigmoid(gates)
    tri = _tri(C, rev)
    cum = _cumsum01(tri.astype(bf16), log_f)
    gates_t = gates.T
    cum_t = cum.T
    d = 1 if rev else 0
    for h in range(HEADS):
        sl = slice(h * DH, (h + 1) * DH)
        li = LANE_ML_I + 4 * d + h
        lf = LANE_ML_F + 4 * d + h
        b_col = cum[:, lf:lf + 1]
        b_row = cum_t[lf:lf + 1, :]
        i_col = gates[:, li:li + 1]
        i_row = gates_t[li:li + 1, :]
        b_end = b_col[0:1, :] if rev else b_col[C - 1:C, :]
        m_prev = m_ref[h][0:1, 0:1]
        s_prev = s_ref[h]
        n_prev = n_ref[h][0:1, :]
        qh = q[:, sl]
        kh = k[:, sl]
        vh = v[:, sl]
        dmat = jnp.where(tri, b_col - b_row + i_row, NEG)
        m_in = b_col + m_prev
        m_t = jnp.maximum(m_in, jnp.max(dmat, axis=1, keepdims=True))
        p = jnp.exp(dmat - m_t) * _bdot_nt(qh, kh)
        a_in = jnp.exp(m_in - m_t)
        num = _bdot(p, vh) + a_in * _bdot(qh, s_prev)
        den = jnp.sum(p, axis=1, keepdims=True) + a_in * jnp.sum(qh * n_prev, axis=1, keepdims=True)
        hd = num / jnp.maximum(jnp.abs(den), jnp.exp(-m_t))
        g_col = b_end - b_col + i_col
        m_loc = jnp.max(g_col, axis=0, keepdims=True)
        kw = kh * jnp.exp(g_col - m_loc)
        m_new = jnp.maximum(b_end + m_prev, m_loc)
        a_old = jnp.exp(b_end + m_prev - m_new)
        a_new = jnp.exp(m_loc - m_new)
        s_ref[h] = a_old * s_prev + a_new * _bdot_tn(kw, vh)
        n_ref[h] = jnp.broadcast_to(a_old * n_prev + a_new * jnp.sum(kw, axis=0, keepdims=True), (8, DH))
        m_ref[h] = jnp.broadcast_to(m_new, (8, DH))
        if final:
            hh = _sigmoid(op_ref[:, sl].astype(f32)) * (hd + hb_ref[:, sl])
            hh = _head_norm(hh, center=True) * ng_ref[:, sl] + skip_ref[:, sl] * xc[:, sl]
            o_ref[:, sl] = (hh * _silu(zg_ref[:, sl].astype(f32))).astype(o_ref.dtype)
        else:
            o_ref[:, sl] = hd


def _ml_branch(proj, small, conv_w, conv_b, wq, wk, wv, bias_small, norm_gain, skip, B, L):
    nc = L // CHUNK
    ns = _seqs_per_step(B)
    per = CHUNK // HALO
    last = L // HALO - 1

    def call(rev, final, extra):
        idx = _chunk_idx(nc, rev)
        prev_spec = pl.BlockSpec((ns, HALO, BR), lambda b, c: (b, jnp.maximum(idx(c) * per - 1, 0), BLK_ML))
        next_spec = pl.BlockSpec((ns, HALO, BR), lambda b, c: (b, jnp.minimum((idx(c) + 1) * per, last), BLK_ML))
        in_specs = [_tok_spec(ns, nc, rev, BLK_ML), prev_spec, next_spec, _tok_spec(ns, nc, rev, 0, N_SMALL),
                    _const_spec((3, BR)), _const_spec((1, BR)),
                    _const_spec((BR, BR)), _const_spec((BR, BR)), _const_spec((BR, BR)),
                    _const_spec((1, 128))]
        args = [proj, proj, proj, small, conv_w, conv_b, wq, wk, wv, bias_small]
        batched = [True] * 4 + [False] * 6
        if final:
            in_specs += [_tok_spec(ns, nc, rev, BLK_ML + 1), _tok_spec(ns, nc, rev, BLK_ML + 2),
                         _tok_spec(ns, nc, rev, 0), _const_spec((1, BR)), _const_spec((1, BR))]
            args += [proj, proj, extra, norm_gain, skip]
            batched += [True, True, True, False, False]
        batched += [True] * 4
        return pl.pallas_call(
            functools.partial(_ml_kernel, rev, final, tuple(batched)),
            grid=(B // ns, nc),
            in_specs=in_specs,
            out_specs=_tok_spec(ns, nc, rev, 0),
            out_shape=jax.ShapeDtypeStruct((B, L, BR), bf16 if final else f32),
            scratch_shapes=[pltpu.VMEM((ns, HEADS, DH, DH), f32), pltpu.VMEM((ns, HEADS, 8, DH), f32),
                            pltpu.VMEM((ns, HEADS, 8, DH), f32)],
            compiler_params=_params(("arbitrary", "arbitrary")),
            name="ml_fwd" if final else "ml_bwd",
        )(*args)

    hb = call(True, False, None)
    return call(False, True, hb)


def _hy_filt_kernel(emb_ref, w1_ref, b1_ref, w2_ref, b2_ref, fr_ref, w3_ref, dl_ref, taps_ref, sc_ref):
    r = pl.program_id(0)

    @pl.when(r == 0)
    def _():
        sc_ref[...] = jnp.zeros_like(sc_ref)

    emb = emb_ref[...]
    fr = fr_ref[...]
    hmid = jnp.sin(fr * (_dot3(emb, w1_ref[...]) + b1_ref[...]))
    hmid = jnp.sin(fr * (_dot3(hmid, w2_ref[...]) + b2_ref[...]))
    hout = _dot3(hmid, w3_ref[0])
    t = emb[:, 0:1]
    valid = emb[:, HY_EMB:HY_EMB + 1]
    taps = hout * jnp.exp(-t * dl_ref[...]) * valid
    taps_ref[...] = taps.astype(taps_ref.dtype)
    sc_ref[...] += jnp.sum(taps * taps, axis=0, keepdims=True)

    @pl.when(r == pl.num_programs(0) - 1)
    def _():
        sc_ref[...] = lax.rsqrt(sc_ref[...] + EPS)


def _hy_filter(emb, w1p, b1, w2, b2, freq, w3d, deltas2, L):
    n = 2 * L
    tr = 512
    half = L // tr
    return pl.pallas_call(
        _hy_filt_kernel,
        grid=(n // tr,),
        in_specs=[
            pl.BlockSpec((tr, HY_EMB_PAD), lambda r: (r, 0)),
            pl.BlockSpec((HY_EMB_PAD, HY_HIDDEN), lambda r: (0, 0)),
            pl.BlockSpec((1, HY_HIDDEN), lambda r: (0, 0)),
            pl.BlockSpec((HY_HIDDEN, HY_HIDDEN), lambda r: (0, 0)),
            pl.BlockSpec((1, HY_HIDDEN), lambda r: (0, 0)),
            pl.BlockSpec((1, HY_HIDDEN), lambda r: (0, 0)),
            pl.BlockSpec((1, HY_HIDDEN, 2 * BR), lambda r: (r // half, 0, 0)),
            pl.BlockSpec((1, 2 * BR), lambda r: (0, 0)),
        ],
        out_specs=[pl.BlockSpec((tr, 2 * BR), lambda r: (r, 0)),
                   pl.BlockSpec((1, 2 * BR), lambda r: (0, 0))],
        out_shape=[jax.ShapeDtypeStruct((n, 2 * BR), bf16), jax.ShapeDtypeStruct((1, 2 * BR), f32)],
        compiler_params=_params(("arbitrary",)),
        name="hy_filter",
    )(emb, w1p, b1, w2, b2, freq, w3d, deltas2)


def _seq_chunk(ref, i, nchunks, w, b):
    r0 = pl.multiple_of(i * FFT_N2, FFT_N2)
    x = ref[pl.ds(r0, FFT_N2), :].astype(f32)
    if w is None:
        return x
    total = nchunks * FFT_N2
    p0 = pl.multiple_of(jnp.maximum(r0 - HALO, 0), HALO)
    n0 = pl.multiple_of(jnp.minimum(r0 + FFT_N2, total - HALO), HALO)
    prev = ref[pl.ds(p0, HALO), :].astype(f32)
    nxt = ref[pl.ds(n0, HALO), :].astype(f32)
    return _conv3(x, prev, nxt, i > 0, i < nchunks - 1, w, b)


def _dft_outer_kernel(K, k1t, conv, *refs):
    if conv:
        x_ref, cw_ref, cb_ref, f_ref, o_ref, xs_ref = refs
        w, b = cw_ref[0], cb_ref[0]
    else:
        x_ref, f_ref, o_ref, xs_ref = refs
        w = b = None

    @pl.when(pl.program_id(2) == 0)
    def _():
        def fill(i, carry):
            xs_ref[pl.ds(pl.multiple_of(i * PITCH_X, 8), FFT_N2), :] = _seq_chunk(x_ref, i, K, w, b)
            return carry
        lax.fori_loop(0, K, fill, 0)

    f = f_ref[0]

    def body(g, carry):
        j0 = g * FFT_NJ
        x = jnp.concatenate([xs_ref[pl.ds(j0 + i, K, stride=PITCH_X), :] for i in range(FFT_NJ)], axis=1)
        r = jnp.dot(f, x.astype(bf16), preferred_element_type=f32)
        for i in range(FFT_NJ):
            sl = slice(i * 128, (i + 1) * 128)
            o_ref[pl.ds(j0 + i, k1t, stride=PITCH_A), :] = r[:k1t, sl]
            o_ref[pl.ds(FFT_N2 + j0 + i, k1t, stride=PITCH_A), :] = r[k1t:, sl]
        return carry

    lax.fori_loop(0, FFT_N2 // FFT_NJ, body, 0, unroll=2)
    zero = jnp.zeros((k1t, 128), f32)
    for i in range(PITCH_A - 2 * FFT_N2):
        o_ref[pl.ds(2 * FFT_N2 + i, k1t, stride=PITCH_A), :] = zero


def _dft_outer(x, col0, ncol, fmat, B, K, K1p, k1t, conv=None):
    nt = K1p // k1t
    in_specs = [pl.BlockSpec((K * FFT_N2, 128), lambda b, c, t: (b, col0 + c))]
    args = [x]
    if conv is not None:
        cw, cb, s = conv
        in_specs += [pl.BlockSpec((1, 3, 128), lambda b, c, t: (s, 0, c)),
                     pl.BlockSpec((1, 1, 128), lambda b, c, t: (s, 0, c))]
        args += [cw, cb]
    in_specs.append(pl.BlockSpec((1, 2 * k1t, K), lambda b, c, t: (t, 0, 0)))
    args.append(fmat)
    return pl.pallas_call(
        functools.partial(_dft_outer_kernel, K, k1t, conv is not None),
        grid=(B, ncol, nt),
        in_specs=in_specs,
        out_specs=pl.BlockSpec((k1t * PITCH_A, 128), lambda b, c, t: (b * nt + t, c)),
        out_shape=jax.ShapeDtypeStruct((B * K1p * PITCH_A, 128 * ncol), f32),
        scratch_shapes=[pltpu.VMEM((K * PITCH_X, 128), f32)],
        compiler_params=_params(("arbitrary", "arbitrary", "arbitrary")),
        name="dft_outer",
    )(*args)


def _dft_inner_kernel(conv, *refs):
    if conv:
        a_ref, gf_ref, h_ref, gi_ref, o_ref = refs
    else:
        a_ref, gf_ref, o_ref = refs
    n2 = FFT_N2
    dot = functools.partial(jnp.dot, preferred_element_type=f32)
    for kk in range(FFT_K1T):
        gf = gf_ref[kk]
        x = dot(gf[:, :n2], a_ref[0, kk, :n2].astype(bf16)) + dot(gf[:, n2:], a_ref[0, kk, n2:2 * n2].astype(bf16))
        if conv:
            xr, xi = x[:n2], x[n2:]
            hr, hi = h_ref[0, kk, :n2], h_ref[0, kk, n2:2 * n2]
            yr = (xr * hr - xi * hi).astype(bf16)
            yi = (xr * hi + xi * hr).astype(bf16)
            gi = gi_ref[kk]
            x = dot(gi[:, :n2], yr) + dot(gi[:, n2:], yi)
        o_ref[0, kk, :2 * n2] = x
        o_ref[0, kk, 2 * n2:] = jnp.zeros((PITCH_A - 2 * n2, BR), f32)


def _dft_inner(a4, gf, h4=None, gi=None, hblk=0):
    B, k1p, pa, W = a4.shape
    conv = h4 is not None
    kt = FFT_K1T
    r2 = 2 * FFT_N2
    in_specs = [pl.BlockSpec((1, kt, pa, BR), lambda k, w, b: (b, k, 0, w)),
                pl.BlockSpec((kt, r2, r2), lambda k, w, b: (k, 0, 0))]
    args = [a4, gf]
    if conv:
        in_specs += [pl.BlockSpec((1, kt, pa, BR), lambda k, w, b: (0, k, 0, hblk)),
                     pl.BlockSpec((kt, r2, r2), lambda k, w, b: (k, 0, 0))]
        args += [h4, gi]
    return pl.pallas_call(
        functools.partial(_dft_inner_kernel, conv),
        grid=(k1p // kt, W // BR, B),
        in_specs=in_specs,
        out_specs=pl.BlockSpec((1, kt, pa, BR), lambda k, w, b: (b, k, 0, w)),
        out_shape=jax.ShapeDtypeStruct((B, k1p, pa, W), f32),
        compiler_params=_params(("arbitrary", "arbitrary", "arbitrary")),
        name="dft_inner_conv" if conv else "dft_inner_spec",
    )(*args)


def _idft_outer_kernel(kh, K1p, final, conv_s, *refs):
    it = iter(refs)
    b_ref, fr_ref, fi_ref, mul_ref, s_ref, sc_ref, sk_ref, cwm_ref, cbm_ref = [next(it) for _ in range(9)]
    cws_ref, cbs_ref = (next(it), next(it)) if conv_s else (None, None)
    g_ref = next(it) if final else None
    o_ref, ys_ref = next(it), next(it)
    fr = fr_ref[...]
    fi = fi_ref[...]
    dot = functools.partial(jnp.dot, preferred_element_type=f32)

    def body(g, carry):
        j0 = g * FFT_NJ
        br = jnp.concatenate([b_ref[pl.ds(j0 + i, K1p, stride=PITCH_A), :] for i in range(FFT_NJ)], axis=1)
        bi = jnp.concatenate([b_ref[pl.ds(FFT_N2 + j0 + i, K1p, stride=PITCH_A), :] for i in range(FFT_NJ)], axis=1)
        y = dot(fr, br.astype(bf16)) + dot(fi, bi.astype(bf16))
        for i in range(FFT_NJ):
            ys_ref[pl.ds(j0 + i, kh, stride=PITCH_X), :] = y[:, i * 128:(i + 1) * 128]
        return carry

    lax.fori_loop(0, FFT_N2 // FFT_NJ, body, 0, unroll=2)
    sc = sc_ref[...]
    sk = sk_ref[...]

    def epi(i, carry):
        y = ys_ref[pl.ds(pl.multiple_of(i * PITCH_X, 8), FFT_N2), :]
        m = _seq_chunk(mul_ref, i, kh, cwm_ref[0], cbm_ref[0])
        s = _seq_chunk(s_ref, i, kh, cws_ref[0], cbs_ref[0]) if conv_s else _seq_chunk(s_ref, i, kh, None, None)
        z = m * (y * sc + s * sk)
        if final:
            z = z * _silu(_seq_chunk(g_ref, i, kh, None, None))
        o_ref[pl.ds(pl.multiple_of(i * FFT_N2, FFT_N2), FFT_N2), :] = z.astype(o_ref.dtype)
        return carry

    lax.fori_loop(0, kh, epi, 0)


def _idft_outer(b2, fr, fi, mul, s, scale, skip, conv_w, conv_b, B, kh, K1p, g=None):
    L = kh * FFT_N2
    once = pl.Buffered(1) if L > 4096 else None
    mul_a, mul_c, mul_s = mul
    s_a, s_c, s_s = s
    conv_s = s_s is not None
    vec = pl.BlockSpec((1, 128), lambda b, c: (0, c))
    in_specs = [pl.BlockSpec((K1p * PITCH_A, 128), lambda b, c: (b, c), pipeline_mode=once),
                pl.BlockSpec((kh, K1p), lambda b, c: (0, 0)), pl.BlockSpec((kh, K1p), lambda b, c: (0, 0)),
                pl.BlockSpec((L, 128), lambda b, c: (b, mul_c + c), pipeline_mode=once),
                pl.BlockSpec((L, 128), lambda b, c: (b, s_c + c), pipeline_mode=once),
                vec, vec,
                pl.BlockSpec((1, 3, 128), lambda b, c: (mul_s, 0, c)),
                pl.BlockSpec((1, 1, 128), lambda b, c: (mul_s, 0, c))]
    args = [b2, fr, fi, mul_a, s_a, scale, skip, conv_w, conv_b]
    if conv_s:
        in_specs += [pl.BlockSpec((1, 3, 128), lambda b, c: (s_s, 0, c)),
                     pl.BlockSpec((1, 1, 128), lambda b, c: (s_s, 0, c))]
        args += [conv_w, conv_b]
    if g is not None:
        g_a, g_c = g
        in_specs.append(pl.BlockSpec((L, 128), lambda b, c: (b, g_c + c), pipeline_mode=once))
        args.append(g_a)
    return pl.pallas_call(
        functools.partial(_idft_outer_kernel, kh, K1p, g is not None, conv_s),
        grid=(B, BR // 128),
        in_specs=in_specs,
        out_specs=pl.BlockSpec((L, 128), lambda b, c: (b, c)),
        out_shape=jax.ShapeDtypeStruct((B * L, BR), bf16),
        scratch_shapes=[pltpu.VMEM((kh * PITCH_X, 128), f32)],
        compiler_params=_params(("arbitrary", "arbitrary")),
        name="idft_outer",
    )(*args)


def _fft_tables(L):
    n = 2 * L
    n2 = FFT_N2
    n1 = n // n2
    kh = n1 // 2
    k1p = -(-(kh + 1) // 16) * 16
    k1t = k1p if k1p <= 48 else 48
    nt = k1p // k1t
    two_pi = 2.0 * math.pi
    k1 = jnp.arange(k1p, dtype=jnp.int32)

    def outer(K):
        m = jnp.arange(K, dtype=jnp.int32)
        ang = (two_pi / n1) * ((k1[:, None] * m[None, :]) % n1).astype(f32)
        re = jnp.cos(ang).reshape(nt, k1t, K)
        im = (-jnp.sin(ang)).reshape(nt, k1t, K)
        return jnp.concatenate([re, im], axis=1).astype(bf16)

    r1 = jnp.arange(kh, dtype=jnp.int32)
    wgt = jnp.where((k1 == 0) | (k1 == kh), 1.0, jnp.where(k1 < kh, 2.0, 0.0)).astype(f32)
    angi = (two_pi / n1) * ((r1[:, None] * k1[None, :]) % n1).astype(f32)
    fi_re = (jnp.cos(angi) * wgt[None, :]).astype(bf16)
    fi_im = (-jnp.sin(angi) * wgt[None, :]).astype(bf16)
    kk = k1[:, None, None]
    a = jnp.arange(n2, dtype=jnp.int32)[None, :, None]
    c = jnp.arange(n2, dtype=jnp.int32)[None, None, :]
    angp = (two_pi / n) * ((a * c * n1 + c * kk) % n).astype(f32)
    mr, mi = jnp.cos(angp), -jnp.sin(angp)
    g_fwd = jnp.concatenate([jnp.concatenate([mr, -mi], axis=2),
                             jnp.concatenate([mi, mr], axis=2)], axis=1).astype(bf16)
    angq = (two_pi / n) * ((a * c * n1 + a * kk) % n).astype(f32)
    vr, vi = jnp.cos(angq), jnp.sin(angq)
    g_inv = jnp.concatenate([jnp.concatenate([vr, -vi], axis=2),
                             jnp.concatenate([vi, vr], axis=2)], axis=1).astype(bf16)
    return dict(n=n, n1=n1, kh=kh, k1p=k1p, k1t=k1t, f_half=outer(kh), f_full=outer(n1),
                fi_re=fi_re, fi_im=fi_im, g_fwd=g_fwd, g_inv=g_inv)


def _hy_embedding(L):
    n = jnp.arange(2 * L, dtype=jnp.int32)
    pos = jnp.where(n < L, n, 2 * L - n).astype(f32)
    t = pos / (L - 1)
    f = jnp.linspace(1e-4, HY_BANDS - 1, HY_BANDS, dtype=f32)
    ang = (2.0 * math.pi / L) * pos[:, None] * f[None, :]
    valid = (n != L).astype(f32)
    pad = jnp.zeros((2 * L, HY_EMB_PAD - HY_EMB - 1), f32)
    return jnp.concatenate([t[:, None], jnp.cos(ang), -jnp.sin(ang), valid[:, None], pad], axis=-1)


def _hy_branch(proj, lw, emb, tabs, B, L):
    n1, kh, k1p, k1t = tabs["n1"], tabs["kh"], tabs["k1p"], tabs["k1t"]
    cw, cb = lw["hy_conv_w"], lw["hy_conv_b"]
    taps, scale = _hy_filter(emb, lw["hy_w1"], lw["hy_b1"], lw["hy_w2"], lw["hy_b2"], lw["hy_freq"],
                             lw["hy_w3"], lw["hy_deltas"], L)
    ta = _dft_outer(taps, 0, 2 * BR // 128, tabs["f_full"], 1, n1, k1p, k1t)
    h4 = _dft_inner(ta.reshape(1, k1p, PITCH_A, 2 * BR), tabs["g_fwd"])
    scale = scale * (1.0 / tabs["n"])
    col = lambda s: (BLK_HY + s) * (BR // 128)

    def conv(x, col0, order, stream):
        a = _dft_outer(x, col0, BR // 128, tabs["f_half"], B, kh, k1p, k1t,
                       None if stream is None else (cw, cb, stream))
        b4 = _dft_inner(a.reshape(B, k1p, PITCH_A, BR), tabs["g_fwd"], h4, tabs["g_inv"], order)
        return b4.reshape(B * k1p * PITCH_A, BR)

    def inv(b2, order, mul, s, g=None):
        return _idft_outer(b2, tabs["fi_re"], tabs["fi_im"], mul, s, scale[:, order * BR:(order + 1) * BR],
                           lw["hy_skip"][order:order + 1], cw, cb, B, kh, k1p, g)

    z = inv(conv(proj, col(0), 0, 0), 0, (proj, col(1), 1), (proj, col(0), 0))
    return inv(conv(z, 0, 1, None), 1, (proj, col(2), 2), (z, 0, None), (proj, col(3)))


def _rotary_tables(L):
    inv = ROPE_BASE ** (-jnp.arange(0, DH, 2, dtype=f32) / DH)
    ang = jnp.arange(L, dtype=f32)[:, None] * inv[None, :]
    cos, sin = jnp.cos(ang), jnp.sin(ang)
    return jnp.concatenate([cos, cos], axis=-1), jnp.concatenate([-sin, sin], axis=-1)


def _block_diag(w):
    n, e, fo = w.shape
    eye = jnp.eye(n, dtype=w.dtype)
    return (eye[:, None, :, None] * w[:, :, None, :]).reshape(n * e, n * fo)


def _permute_w_in(w_in):
    ref_gla_small = 2048 + 1536
    ref_hy = ref_gla_small + 2 * GLA_RANK
    ref_ml = ref_hy + 2048
    ref_ml_small = ref_ml + 1536
    main = jnp.concatenate([w_in[..., :ref_gla_small], w_in[..., ref_hy:ref_ml], w_in[..., ref_ml:ref_ml_small]],
                           axis=-1)
    small = jnp.concatenate([w_in[..., ref_gla_small:ref_hy], w_in[..., ref_ml_small:]], axis=-1)
    small = jnp.pad(small, [(0, 0)] * (small.ndim - 1) + [(0, N_SMALL - small.shape[-1])])
    return main.astype(bf16), small.astype(bf16)


def _layer_weights(l, p):
    lw = {}
    lw["gla_wa"] = jnp.stack([
        jnp.pad(p["gla_w_a"][l, d], ((LANE_GLA_A + GLA_RANK * d, 128 - LANE_GLA_A - GLA_RANK * (d + 1)), (0, 0)))
        for d in range(2)]).astype(bf16)
    lw["gla_ba"] = p["gla_b_a"][l]
    lw["gla_ng"] = p["gla_norm_gain"][l].reshape(1, DH)
    lw["hy_conv_w"] = p["hy_conv_w"][l].reshape(3, 3, BR).transpose(1, 0, 2)
    lw["hy_conv_b"] = p["hy_conv_b"][l].reshape(3, 1, BR)
    lw["hy_w1"] = jnp.pad(p["hy_w1"][l], ((0, HY_EMB_PAD - HY_EMB), (0, 0)))
    lw["hy_b1"] = p["hy_b1"][l].reshape(1, HY_HIDDEN)
    lw["hy_w2"] = p["hy_w2"][l]
    lw["hy_b2"] = p["hy_b2"][l].reshape(1, HY_HIDDEN)
    lw["hy_freq"] = p["hy_freq"][l].reshape(1, HY_HIDDEN)
    lw["hy_w3"] = p["hy_w3"][l].reshape(HY_HIDDEN, 2, 2, BR).transpose(2, 0, 1, 3).reshape(2, HY_HIDDEN, 2 * BR)
    deltas = jnp.abs(jnp.linspace(HY_MIN_DECAY, HY_MAX_DECAY, BR, dtype=f32))
    lw["hy_deltas"] = jnp.concatenate([deltas, deltas]).reshape(1, 2 * BR)
    lw["hy_skip"] = p["hy_skip"][l]
    lw["ml_conv_w"] = p["ml_conv_w"][l]
    lw["ml_conv_b"] = p["ml_conv_b"][l].reshape(1, BR)
    lw["ml_wq"] = _block_diag(p["ml_wq"][l]).astype(bf16)
    lw["ml_wk"] = _block_diag(p["ml_wk"][l]).astype(bf16)
    lw["ml_wv"] = _block_diag(p["ml_wv"][l]).astype(bf16)
    bias = jnp.concatenate([p["ml_b_i"][l].reshape(-1), p["ml_b_f"][l].reshape(-1)])
    lw["ml_bias"] = jnp.pad(bias, (LANE_ML_I, 128 - LANE_ML_I - bias.shape[0])).reshape(1, 128)
    lw["ml_ng"] = p["ml_norm_gain"][l].reshape(1, BR)
    lw["ml_skip"] = p["ml_skip"][l].reshape(1, BR)
    return lw


def _mix(proj, small, lw, consts, B, L):
    cosf, sins, emb, tabs = consts
    proj3 = proj.reshape(B, L, N_PROJ)
    small3 = small.reshape(B, L, N_SMALL)
    return (
        _ret_branch(proj3, cosf, sins, B, L).reshape(B * L, BR),
        _gla_branch(proj3, small3, lw["gla_wa"], lw["gla_ba"], lw["gla_ng"], B, L).reshape(B * L, BR),
        _hy_branch(proj, lw, emb, tabs, B, L),
        _ml_branch(proj3, small3, lw["ml_conv_w"], lw["ml_conv_b"], lw["ml_wq"], lw["ml_wk"], lw["ml_wv"],
                   lw["ml_bias"], lw["ml_ng"], lw["ml_skip"], B, L).reshape(B * L, BR),
    )


def kernel(x_prompt, x_sample, c_prompt, c_sample, norm_gain, w_ada, b_ada, w_in, gla_w_a, gla_b_a, gla_norm_gain, hy_conv_w, hy_conv_b, hy_w1, hy_b1, hy_w2, hy_b2, hy_freq, hy_w3, hy_skip, ml_conv_w, ml_conv_b, ml_wq, ml_wk, ml_wv, ml_b_i, ml_b_f, ml_norm_gain, ml_skip, w_branch, w_gate, b_gate, w_out, final_gain):
    p = dict(gla_w_a=gla_w_a, gla_b_a=gla_b_a, gla_norm_gain=gla_norm_gain, hy_conv_w=hy_conv_w,
             hy_conv_b=hy_conv_b, hy_w1=hy_w1, hy_b1=hy_b1, hy_w2=hy_w2, hy_b2=hy_b2, hy_freq=hy_freq,
             hy_w3=hy_w3, hy_skip=hy_skip, ml_conv_w=ml_conv_w, ml_conv_b=ml_conv_b, ml_wq=ml_wq,
             ml_wk=ml_wk, ml_wv=ml_wv, ml_b_i=ml_b_i, ml_b_f=ml_b_f, ml_norm_gain=ml_norm_gain,
             ml_skip=ml_skip)
    groups = []
    for x, c in ((x_prompt, c_prompt), (x_sample, c_sample)):
        B, L, _ = x.shape
        groups.append(dict(B=B, L=L, x=x.reshape(B * L, D_MODEL),
                           consts=_rotary_tables(L) + (_hy_embedding(L), _fft_tables(L))))
    nb = [g["B"] for g in groups]
    rows = -(-sum(nb) // 8) * 8
    c_all = jnp.concatenate([c_prompt, c_sample, jnp.zeros((rows - sum(nb), D_MODEL), f32)], axis=0)
    mod_all = _ada_all(c_all, w_ada, b_ada)

    w_in_p, w_in_s = _permute_w_in(w_in)
    wg = w_gate.astype(bf16)
    wb = w_branch.astype(bf16)
    wo = w_out.astype(bf16)
    fg = final_gain.reshape(1, D_MODEL)
    for l in range(DEPTH):
        lw = _layer_weights(l, p)
        gain = norm_gain[l].reshape(1, D_MODEL)
        bg = b_gate[l].reshape(4, 1, D_MODEL)
        start = 0
        for g in groups:
            B, L = g["B"], g["L"]
            mod = mod_all[l, start:start + B].reshape(B, 3, D_MODEL)
            start += B
            proj, small = _inproj(g["x"], mod, gain, w_in_p[l], w_in_s[l], L)
            branches = _mix(proj, small, lw, g["consts"], B, L)
            g["x"] = _merge(g["x"], mod, gain, branches, wg[l], bg, wb[l], wo[l], fg, L, l == DEPTH - 1)
    return tuple(g["x"].reshape(g["B"], g["L"], D_MODEL) for g in groups)
```

```python
import functools
import math

import jax
import jax.numpy as jnp
from jax import lax
from jax.experimental import pallas as pl
from jax.experimental.pallas import tpu as pltpu

f32 = jnp.float32
bf16 = jnp.bfloat16

D_MODEL = 2048
DEPTH = 4
BR = 512
HEADS = 4
DH = 128
ROPE_BASE = 10000.0
GLA_DK = 64
GLA_KW = 256
GLA_RANK = 16
GLA_GATE_NORM = 16.0
HY_EMB = 33
HY_EMB_PAD = 40
HY_BANDS = 16
HY_HIDDEN = 64
HY_MIN_DECAY = math.log(1e-2) / 1.5
HY_MAX_DECAY = math.log(1e-2) / 0.3
ML_BLOCK = 4
EPS = 1e-6
NEG = -1e30

BLK_RET = 0
BLK_GLA_QK = 4
BLK_GLA_V = 5
BLK_GLA_G = 6
BLK_HY = 7
BLK_ML = 11
N_PROJ = 7168
N_SMALL = 128
LANE_GLA_A = 0
LANE_ML_I = 32
LANE_ML_F = 40

CHUNK = 128
HALO = 16
FFT_N2 = 128
FFT_K1T = 8
FFT_NJ = 4
PITCH_X = FFT_N2 + 8
PITCH_A = 2 * FFT_N2 + 8
TM_IN = 2048
TN_IN = 512
TM_GATES = 1024
TN_GATES = 512
TM_OUT = 512
SEQ_PER_STEP = 2
VMEM_LIMIT = 60 * 1024 * 1024

LOG_GAMMA = tuple(math.log(1.0 - 2.0 ** (-5.0 - h)) for h in range(HEADS))


def _params(sem):
    return pltpu.CompilerParams(dimension_semantics=sem, vmem_limit_bytes=VMEM_LIMIT)


def _bdot(a, b):
    return jnp.dot(a.astype(bf16), b.astype(bf16), preferred_element_type=f32)


def _bdot_nt(a, b):
    return lax.dot_general(a.astype(bf16), b.astype(bf16), (((1,), (1,)), ((), ())),
                           preferred_element_type=f32)


def _bdot_tn(a, b):
    return jnp.dot(a.T.astype(bf16), b.astype(bf16), preferred_element_type=f32)


def _dot3(a, b):
    a1 = a.astype(bf16)
    ra = a - a1.astype(f32)
    a2 = ra.astype(bf16)
    a3 = (ra - a2.astype(f32)).astype(bf16)
    b1 = b.astype(bf16)
    rb = b - b1.astype(f32)
    b2 = rb.astype(bf16)
    b3 = (rb - b2.astype(f32)).astype(bf16)
    d = functools.partial(jnp.dot, preferred_element_type=f32)
    return (d(a1, b1) + (d(a1, b2) + d(a2, b1))) + ((d(a2, b2) + d(a1, b3)) + d(a3, b1))


def _cumsum01(tri, x):
    x1 = x.astype(bf16)
    r = x - x1.astype(f32)
    x2 = r.astype(bf16)
    x3 = (r - x2.astype(f32)).astype(bf16)
    d = functools.partial(jnp.dot, preferred_element_type=f32)
    return d(tri, x1) + (d(tri, x2) + d(tri, x3))


def _sigmoid(x):
    return jax.nn.sigmoid(x)


def _silu(x):
    return x * jax.nn.sigmoid(x)


def _log_sigmoid(x):
    return jnp.minimum(x, 0.0) - jnp.log1p(jnp.exp(-jnp.abs(x)))


def _tri(n, rev):
    ii = lax.broadcasted_iota(jnp.int32, (n, n), 0)
    jj = lax.broadcasted_iota(jnp.int32, (n, n), 1)
    return (jj >= ii) if rev else (jj <= ii)


def _prev_rows(x, first):
    row = lax.broadcasted_iota(jnp.int32, x.shape, 0)
    return jnp.where(row == 0, first, pltpu.roll(x, 1, 0))


def _next_rows(x, last):
    n = x.shape[0]
    row = lax.broadcasted_iota(jnp.int32, x.shape, 0)
    return jnp.where(row == n - 1, last, pltpu.roll(x, n - 1, 0))


def _conv3(x, prev, nxt, has_prev, has_next, w, b):
    first = jnp.where(has_prev, prev[HALO - 1:HALO, :], 0.0)
    last = jnp.where(has_next, nxt[0:1, :], 0.0)
    return _prev_rows(x, first) * w[0:1] + x * w[1:2] + _next_rows(x, last) * w[2:3] + b


def _head_norm(x, center):
    if center:
        x = x - jnp.mean(x, axis=-1, keepdims=True)
    return x * lax.rsqrt(jnp.mean(x * x, axis=-1, keepdims=True) + EPS)


def _ada_kernel(c_ref, w_ref, b_ref, o_ref):
    o_ref[0] = _bdot(_silu(c_ref[...]), w_ref[0]) + b_ref[0]


def _ada_all(c_all, w_ada, b_ada):
    rows = c_all.shape[0]
    tn = 1536
    return pl.pallas_call(
        _ada_kernel,
        grid=(DEPTH, 3 * D_MODEL // tn),
        in_specs=[
            pl.BlockSpec((rows, D_MODEL), lambda l, j: (0, 0)),
            pl.BlockSpec((1, D_MODEL, tn), lambda l, j: (l, 0, j)),
            pl.BlockSpec((1, 1, tn), lambda l, j: (l, 0, j)),
        ],
        out_specs=pl.BlockSpec((1, rows, tn), lambda l, j: (l, 0, j)),
        out_shape=jax.ShapeDtypeStruct((DEPTH, rows, 3 * D_MODEL), f32),
        compiler_params=_params(("arbitrary", "arbitrary")),
        name="ada",
    )(c_all, w_ada, b_ada.reshape(DEPTH, 1, 3 * D_MODEL))


ROWS_NORM = 256


def _modulated_norm(x, mod, gain):
    y = x * lax.rsqrt(jnp.mean(x * x, axis=-1, keepdims=True) + EPS) * gain
    return y * (1.0 + mod[1:2]) + mod[0:1]


def _norm_to_scratch(x_ref, mod_ref, gain_ref, h_ref):
    mod = mod_ref[0]
    gain = gain_ref[...]

    def step(i, carry):
        rows = pl.ds(pl.multiple_of(i * ROWS_NORM, ROWS_NORM), ROWS_NORM)
        h_ref[rows, :] = _modulated_norm(x_ref[rows, :], mod, gain).astype(bf16)
        return carry

    lax.fori_loop(0, x_ref.shape[0] // ROWS_NORM, step, 0)


def _inproj_kernel(x_ref, mod_ref, gain_ref, w_ref, ws_ref, o_ref, os_ref, h_ref):
    @pl.when(pl.program_id(1) == 0)
    def _():
        _norm_to_scratch(x_ref, mod_ref, gain_ref, h_ref)
        os_ref[...] = jnp.dot(h_ref[...], ws_ref[...], preferred_element_type=f32)

    o_ref[...] = jnp.dot(h_ref[...], w_ref[0], preferred_element_type=f32).astype(bf16)


def _inproj(x, mod, gain, w, ws, L):
    T = x.shape[0]
    tm = min(TM_IN, L)
    return pl.pallas_call(
        _inproj_kernel,
        grid=(T // tm, N_PROJ // TN_IN),
        in_specs=[
            pl.BlockSpec((tm, D_MODEL), lambda i, j: (i, 0), pipeline_mode=pl.Buffered(1)),
            pl.BlockSpec((1, 3, D_MODEL), lambda i, j: ((i * tm) // L, 0, 0)),
            pl.BlockSpec((1, D_MODEL), lambda i, j: (0, 0)),
            pl.BlockSpec((1, D_MODEL, TN_IN), lambda i, j: (j, 0, 0)),
            pl.BlockSpec((D_MODEL, N_SMALL), lambda i, j: (0, 0)),
        ],
        out_specs=[pl.BlockSpec((tm, TN_IN), lambda i, j: (i, j)),
                   pl.BlockSpec((tm, N_SMALL), lambda i, j: (i, 0)),
                   pl.BlockSpec((tm, D_MODEL), lambda i, j: (i, 0))],
        out_shape=[jax.ShapeDtypeStruct((T, N_PROJ), bf16), jax.ShapeDtypeStruct((T, N_SMALL), f32),
                   jax.ShapeDtypeStruct((T, D_MODEL), bf16)],
        compiler_params=_params(("arbitrary", "arbitrary")),
        name="inproj",
    )(x, mod, gain, w, ws)


def _gates_kernel(h_ref, b0_ref, b1_ref, b2_ref, b3_ref, wg_ref, bg_ref, wb_ref, o_ref):
    h = h_ref[...]
    merged = None
    for bi, br_ref in enumerate((b0_ref, b1_ref, b2_ref, b3_ref)):
        g = _sigmoid(jnp.dot(h, wg_ref[0, bi], preferred_element_type=f32) + bg_ref[0, bi])
        t = g * jnp.dot(br_ref[...], wb_ref[0, bi], preferred_element_type=f32)
        merged = t if merged is None else merged + t
    o_ref[...] = merged.astype(bf16)


def _gated_merge(h, branches, wg, bg, wb, L):
    T = h.shape[0]
    tm = min(TM_GATES, L)
    nj, _, _, tn = wg.shape
    br_spec = pl.BlockSpec((tm, BR), lambda i, j: (i, 0))
    return pl.pallas_call(
        _gates_kernel,
        grid=(T // tm, nj),
        in_specs=[
            pl.BlockSpec((tm, D_MODEL), lambda i, j: (i, 0)),
            br_spec, br_spec, br_spec, br_spec,
            pl.BlockSpec((1, 4, D_MODEL, tn), lambda i, j: (j, 0, 0, 0)),
            pl.BlockSpec((1, 4, 1, tn), lambda i, j: (j, 0, 0, 0)),
            pl.BlockSpec((1, 4, BR, tn), lambda i, j: (j, 0, 0, 0)),
        ],
        out_specs=pl.BlockSpec((tm, tn), lambda i, j: (i, j)),
        out_shape=jax.ShapeDtypeStruct((T, D_MODEL), bf16),
        compiler_params=_params(("arbitrary", "arbitrary")),
        name="gates",
    )(h, *branches, wg, bg, wb)


def _outproj_kernel(final, m_ref, wo_ref, x_ref, mod_ref, fg_ref, o_ref):
    y = x_ref[...] + mod_ref[0][2:3] * jnp.dot(m_ref[...], wo_ref[...], preferred_element_type=f32)
    if final:
        y = y * lax.rsqrt(jnp.mean(y * y, axis=-1, keepdims=True) + EPS) * fg_ref[...]
    o_ref[...] = y


def _outproj(merged, wo, x, mod, final_gain, L, final):
    T = x.shape[0]
    tm = min(TM_OUT, L)
    tile = pl.BlockSpec((tm, D_MODEL), lambda i: (i, 0))
    return pl.pallas_call(
        functools.partial(_outproj_kernel, final),
        grid=(T // tm,),
        in_specs=[tile,
                  pl.BlockSpec((D_MODEL, D_MODEL), lambda i: (0, 0), pipeline_mode=pl.Buffered(1)),
                  tile,
                  pl.BlockSpec((1, 3, D_MODEL), lambda i: ((i * tm) // L, 0, 0)),
                  pl.BlockSpec((1, D_MODEL), lambda i: (0, 0))],
        out_specs=tile,
        out_shape=jax.ShapeDtypeStruct((T, D_MODEL), f32),
        compiler_params=_params(("arbitrary",)),
        name="outproj",
    )(merged, wo, x, mod, final_gain)


def _chunk_idx(nc, rev):
    if rev:
        return lambda c: nc - 1 - c
    return lambda c: c


def _tok_spec(ns, nc, rev, blk, width=BR):
    idx = _chunk_idx(nc, rev)
    return pl.BlockSpec((ns, CHUNK, width), lambda b, c: (b, idx(c), blk))


def _per_sequence(body, batched, *refs):
    ns = [r for r, f in zip(refs, batched) if f][0].shape[0]
    for s in range(ns):
        body(*[r.at[s] if f else r for r, f in zip(refs, batched)])


def _seqs_per_step(B):
    return SEQ_PER_STEP if B % SEQ_PER_STEP == 0 else 1


def _pos_spec(nc, rev):
    if rev:
        return pl.BlockSpec((CHUNK, DH), lambda b, c: (nc - 1 - c, 0))
    return pl.BlockSpec((CHUNK, DH), lambda b, c: (c, 0))


def _const_spec(shape):
    nd = len(shape)
    return pl.BlockSpec(shape, lambda b, c: (0,) * nd)


def _ret_kernel(rev, final, batched, *refs):
    _per_sequence(functools.partial(_ret_body, rev, final), batched, *refs)


def _ret_body(rev, final, *refs):
    if final:
        q_ref, k_ref, v_ref, cos_ref, sin_ref, g_ref, rb_ref, o_ref, s_ref = refs
    else:
        q_ref, k_ref, v_ref, cos_ref, sin_ref, o_ref, s_ref = refs
    C = CHUNK

    @pl.when(pl.program_id(1) == 0)
    def _():
        s_ref[...] = jnp.zeros_like(s_ref)

    cos = cos_ref[...]
    sin = sin_ref[...]
    ii = lax.broadcasted_iota(jnp.int32, (C, C), 0)
    jj = lax.broadcasted_iota(jnp.int32, (C, C), 1)
    col = lax.broadcasted_iota(jnp.int32, (C, 1), 0).astype(f32)
    dist = (ii - jj).astype(f32)
    for h in range(HEADS):
        lg = LOG_GAMMA[h]
        sl = slice(h * DH, (h + 1) * DH)
        q = q_ref[:, sl].astype(f32)
        k = k_ref[:, sl].astype(f32)
        v = v_ref[:, sl]
        q = q * cos + pltpu.roll(q, DH // 2, 1) * sin
        k = (k * cos + pltpu.roll(k, DH // 2, 1) * sin) * (DH ** -0.5)
        if rev:
            gam = jnp.where(jj > ii, jnp.exp(-dist * lg), 0.0)
            qd = jnp.exp((C - col) * lg)
            kd = jnp.exp(col * lg)
        else:
            gam = jnp.where(jj <= ii, jnp.exp(dist * lg), 0.0)
            qd = jnp.exp((col + 1.0) * lg)
            kd = jnp.exp((C - 1.0 - col) * lg)
        s_prev = s_ref[h]
        o = _bdot(_bdot_nt(q, k) * gam, v) + qd * _bdot(q, s_prev)
        s_ref[h] = math.exp(C * lg) * s_prev + _bdot_tn(k * kd, v)
        if final:
            r = _head_norm(o + rb_ref[:, sl], center=True)
            o_ref[:, sl] = (r * _silu(g_ref[:, sl].astype(f32))).astype(o_ref.dtype)
        else:
            o_ref[:, sl] = o


def _ret_branch(proj, cosf, sins, B, L):
    nc = L // CHUNK
    ns = _seqs_per_step(B)

    def call(rev, final, extra):
        in_specs = [_tok_spec(ns, nc, rev, BLK_RET + i) for i in range(3)]
        in_specs += [_pos_spec(nc, rev), _pos_spec(nc, rev)]
        args = [proj, proj, proj, cosf, sins]
        batched = [True, True, True, False, False]
        if final:
            in_specs += [_tok_spec(ns, nc, rev, BLK_RET + 3), _tok_spec(ns, nc, rev, 0)]
            args += [proj, extra]
            batched += [True, True]
        batched += [True, True]
        return pl.pallas_call(
            functools.partial(_ret_kernel, rev, final, tuple(batched)),
            grid=(B // ns, nc),
            in_specs=in_specs,
            out_specs=_tok_spec(ns, nc, rev, 0),
            out_shape=jax.ShapeDtypeStruct((B, L, BR), bf16 if final else f32),
            scratch_shapes=[pltpu.VMEM((ns, HEADS, DH, DH), f32)],
            compiler_params=_params(("arbitrary", "arbitrary")),
            name="ret_fwd" if final else "ret_bwd",
        )(*args)

    rb = call(True, False, None)
    return call(False, True, rb)


def _gla_kernel(rev, final, batched, *refs):
    _per_sequence(functools.partial(_gla_body, rev, final), batched, *refs)


def _gla_body(rev, final, *refs):
    if final:
        qk_ref, v_ref, small_ref, wa_ref, ba_ref, g_ref, ob_ref, ng_ref, o_ref, st_ref = refs
    else:
        qk_ref, v_ref, small_ref, wa_ref, ba_ref, o_ref, st_ref = refs
    C = CHUNK

    @pl.when(pl.program_id(1) == 0)
    def _():
        st_ref[...] = jnp.zeros_like(st_ref)

    q = qk_ref[:, :GLA_KW].astype(f32) * (GLA_DK ** -0.5)
    k = qk_ref[:, GLA_KW:].astype(f32)
    v = v_ref[...].astype(f32)
    tri = _tri(C, rev)
    log_a = _log_sigmoid(_bdot(small_ref[...], wa_ref[...]) + ba_ref[...]) / GLA_GATE_NORM
    b = _cumsum01(tri.astype(bf16), log_a)
    ref_row = b[C // 2:C // 2 + 1, :]
    b_end = b[0:1, :] if rev else b[C - 1:C, :]
    qt = q * jnp.exp(b - ref_row)
    kt = k * jnp.exp(ref_row - b)
    inter = _bdot_nt(q * jnp.exp(b), st_ref[...])
    lane = lax.broadcasted_iota(jnp.int32, (1, GLA_KW), 1)
    for h in range(HEADS):
        sl = slice(h * DH, (h + 1) * DH)
        a = _bdot_nt(jnp.where(lane // GLA_DK == h, qt, 0.0), kt)
        o = _bdot(jnp.where(tri, a, 0.0), v[:, sl]) + inter[:, sl]
        if final:
            o = _head_norm(o + ob_ref[:, sl], center=False) * ng_ref[...]
            o_ref[:, sl] = (o * _silu(g_ref[:, sl].astype(f32))).astype(o_ref.dtype)
        else:
            o_ref[:, sl] = o
    ee = lax.broadcasted_iota(jnp.int32, (BR, GLA_KW), 0) // DH
    dd = lax.broadcasted_iota(jnp.int32, (BR, GLA_KW), 1) // GLA_DK
    upd = _bdot_tn(v, k * jnp.exp(b_end - b))
    st_ref[...] = st_ref[...] * jnp.exp(b_end) + jnp.where(ee == dd, upd, 0.0)


def _gla_branch(proj, small, wa_pad, ba, norm_gain, B, L):
    nc = L // CHUNK
    ns = _seqs_per_step(B)

    def call(rev, final, extra):
        d = 1 if rev else 0
        in_specs = [_tok_spec(ns, nc, rev, BLK_GLA_QK), _tok_spec(ns, nc, rev, BLK_GLA_V),
                    _tok_spec(ns, nc, rev, 0, N_SMALL),
                    _const_spec((128, GLA_KW)), _const_spec((1, GLA_KW))]
        args = [proj, proj, small, wa_pad[d], ba[d:d + 1]]
        batched = [True, True, True, False, False]
        if final:
            in_specs += [_tok_spec(ns, nc, rev, BLK_GLA_G), _tok_spec(ns, nc, rev, 0), _const_spec((1, DH))]
            args += [proj, extra, norm_gain]
            batched += [True, True, False]
        batched += [True, True]
        return pl.pallas_call(
            functools.partial(_gla_kernel, rev, final, tuple(batched)),
            grid=(B // ns, nc),
            in_specs=in_specs,
            out_specs=_tok_spec(ns, nc, rev, 0),
            out_shape=jax.ShapeDtypeStruct((B, L, BR), bf16 if final else f32),
            scratch_shapes=[pltpu.VMEM((ns, BR, GLA_KW), f32)],
            compiler_params=_params(("arbitrary", "arbitrary")),
            name="gla_fwd" if final else "gla_bwd",
        )(*args)

    ob = call(True, False, None)
    return call(False, True, ob)


def _ml_kernel(rev, final, batched, *refs):
    _per_sequence(functools.partial(_ml_body, rev, final), batched, *refs)


def _ml_body(rev, final, *refs):
    if final:
        (u_ref, up_ref, un_ref, small_ref, cw_ref, cb_ref, wq_ref, wk_ref, wv_ref, bias_ref,
         zg_ref, op_ref, hb_ref, ng_ref, skip_ref, o_ref, s_ref, n_ref, m_ref) = refs
    else:
        (u_ref, up_ref, un_ref, small_ref, cw_ref, cb_ref, wq_ref, wk_ref, wv_ref, bias_ref,
         o_ref, s_ref, n_ref, m_ref) = refs
    C = CHUNK
    c = pl.program_id(1)
    nc = pl.num_programs(1)

    @pl.when(c == 0)
    def _():
        s_ref[...] = jnp.zeros_like(s_ref)
        n_ref[...] = jnp.zeros_like(n_ref)
        m_ref[...] = jnp.full(m_ref.shape, NEG, f32)

    pos = (nc - 1 - c) if rev else c
    u = u_ref[...].astype(f32)
    xc = _silu(_conv3(u, up_ref[...].astype(f32), un_ref[...].astype(f32), pos > 0, pos < nc - 1,
                      cw_ref[...], cb_ref[...]))
    q = _bdot(xc, wq_ref[...])
    k = _bdot(xc, wk_ref[...]) * (DH ** -0.5)
    v = _bdot(u, wv_ref[...])

    gates = small_ref[...] + bias_ref[...]
    log_f = _log_sigmoid(gates)
    tri = _tri(C, rev)
    cum = _cumsum01(tri.astype(bf16), log_f)
    gates_t = gates.T
    cum_t = cum.T
    d = 1 if rev else 0
    for h in range(HEADS):
        sl = slice(h * DH, (h + 1) * DH)
        li = LANE_ML_I + 4 * d + h
        lf = LANE_ML_F + 4 * d + h
        b_col = cum[:, lf:lf + 1]
        b_row = cum_t[lf:lf + 1, :]
        i_col = gates[:, li:li + 1]
        i_row = gates_t[li:li + 1, :]
        b_end = b_col[0:1, :] if rev else b_col[C - 1:C, :]
        m_prev = m_ref[h][0:1, 0:1]
        s_prev = s_ref[h]
        n_prev = n_ref[h][0:1, :]
        qh = q[:, sl]
        kh = k[:, sl]
        vh = v[:, sl]
        dmat = jnp.where(tri, b_col - b_row + i_row, NEG)
        m_in = b_col + m_prev
        m_t = jnp.maximum(m_in, jnp.max(dmat, axis=1, keepdims=True))
        p = jnp.exp(dmat - m_t) * _bdot_nt(qh, kh)
        a_in = jnp.exp(m_in - m_t)
        num = _bdot(p, vh) + a_in * _bdot(qh, s_prev)
        den = jnp.sum(p, axis=1, keepdims=True) + a_in * jnp.sum(qh * n_prev, axis=1, keepdims=True)
        hd = num / jnp.maximum(jnp.abs(den), jnp.exp(-m_t))
        g_col = b_end - b_col + i_col
        m_loc = jnp.max(g_col, axis=0, keepdims=True)
        kw = kh * jnp.exp(g_col - m_loc)
        m_new = jnp.maximum(b_end + m_prev, m_loc)
        a_old = jnp.exp(b_end + m_prev - m_new)
        a_new = jnp.exp(m_loc - m_new)
        s_ref[h] = a_old * s_prev + a_new * _bdot_tn(kw, vh)
        n_ref[h] = jnp.broadcast_to(a_old * n_prev + a_new * jnp.sum(kw, axis=0, keepdims=True), (8, DH))
        m_ref[h] = jnp.broadcast_to(m_new, (8, DH))
        if final:
            hh = _sigmoid(op_ref[:, sl].astype(f32)) * (hd + hb_ref[:, sl])
            hh = _head_norm(hh, center=True) * ng_ref[:, sl] + skip_ref[:, sl] * xc[:, sl]
            o_ref[:, sl] = (hh * _silu(zg_ref[:, sl].astype(f32))).astype(o_ref.dtype)
        else:
            o_ref[:, sl] = hd


def _ml_branch(proj, small, conv_w, conv_b, wq, wk, wv, bias_small, norm_gain, skip, B, L):
    nc = L // CHUNK
    ns = _seqs_per_step(B)
    per = CHUNK // HALO
    last = L // HALO - 1

    def call(rev, final, extra):
        idx = _chunk_idx(nc, rev)
        prev_spec = pl.BlockSpec((ns, HALO, BR), lambda b, c: (b, jnp.maximum(idx(c) * per - 1, 0), BLK_ML))
        next_spec = pl.BlockSpec((ns, HALO, BR), lambda b, c: (b, jnp.minimum((idx(c) + 1) * per, last), BLK_ML))
        in_specs = [_tok_spec(ns, nc, rev, BLK_ML), prev_spec, next_spec, _tok_spec(ns, nc, rev, 0, N_SMALL),
                    _const_spec((3, BR)), _const_spec((1, BR)),
                    _const_spec((BR, BR)), _const_spec((BR, BR)), _const_spec((BR, BR)),
                    _const_spec((1, 128))]
        args = [proj, proj, proj, small, conv_w, conv_b, wq, wk, wv, bias_small]
        batched = [True] * 4 + [False] * 6
        if final:
            in_specs += [_tok_spec(ns, nc, rev, BLK_ML + 1), _tok_spec(ns, nc, rev, BLK_ML + 2),
                         _tok_spec(ns, nc, rev, 0), _const_spec((1, BR)), _const_spec((1, BR))]
            args += [proj, proj, extra, norm_gain, skip]
            batched += [True, True, True, False, False]
        batched += [True] * 4
        return pl.pallas_call(
            functools.partial(_ml_kernel, rev, final, tuple(batched)),
            grid=(B // ns, nc),
            in_specs=in_specs,
            out_specs=_tok_spec(ns, nc, rev, 0),
            out_shape=jax.ShapeDtypeStruct((B, L, BR), bf16 if final else f32),
            scratch_shapes=[pltpu.VMEM((ns, HEADS, DH, DH), f32), pltpu.VMEM((ns, HEADS, 8, DH), f32),
                            pltpu.VMEM((ns, HEADS, 8, DH), f32)],
            compiler_params=_params(("arbitrary", "arbitrary")),
            name="ml_fwd" if final else "ml_bwd",
        )(*args)

    hb = call(True, False, None)
    return call(False, True, hb)


def _hy_filt_kernel(emb_ref, w1_ref, b1_ref, w2_ref, b2_ref, fr_ref, w3_ref, dl_ref, taps_ref, sc_ref):
    r = pl.program_id(0)

    @pl.when(r == 0)
    def _():
        sc_ref[...] = jnp.zeros_like(sc_ref)

    emb = emb_ref[...]
    fr = fr_ref[...]
    hmid = jnp.sin(fr * (_dot3(emb, w1_ref[...]) + b1_ref[...]))
    hmid = jnp.sin(fr * (_dot3(hmid, w2_ref[...]) + b2_ref[...]))
    hout = _dot3(hmid, w3_ref[0])
    t = emb[:, 0:1]
    valid = emb[:, HY_EMB:HY_EMB + 1]
    taps = hout * jnp.exp(-t * dl_ref[...]) * valid
    taps_ref[...] = taps.astype(taps_ref.dtype)
    sc_ref[...] += jnp.sum(taps * taps, axis=0, keepdims=True)

    @pl.when(r == pl.num_programs(0) - 1)
    def _():
        sc_ref[...] = lax.rsqrt(sc_ref[...] + EPS)


def _hy_filter(emb, w1p, b1, w2, b2, freq, w3d, deltas2, L):
    n = 2 * L
    tr = 512
    half = L // tr
    return pl.pallas_call(
        _hy_filt_kernel,
        grid=(n // tr,),
        in_specs=[
            pl.BlockSpec((tr, HY_EMB_PAD), lambda r: (r, 0)),
            pl.BlockSpec((HY_EMB_PAD, HY_HIDDEN), lambda r: (0, 0)),
            pl.BlockSpec((1, HY_HIDDEN), lambda r: (0, 0)),
            pl.BlockSpec((HY_HIDDEN, HY_HIDDEN), lambda r: (0, 0)),
            pl.BlockSpec((1, HY_HIDDEN), lambda r: (0, 0)),
            pl.BlockSpec((1, HY_HIDDEN), lambda r: (0, 0)),
            pl.BlockSpec((1, HY_HIDDEN, 2 * BR), lambda r: (r // half, 0, 0)),
            pl.BlockSpec((1, 2 * BR), lambda r: (0, 0)),
        ],
        out_specs=[pl.BlockSpec((tr, 2 * BR), lambda r: (r, 0)),
                   pl.BlockSpec((1, 2 * BR), lambda r: (0, 0))],
        out_shape=[jax.ShapeDtypeStruct((n, 2 * BR), bf16), jax.ShapeDtypeStruct((1, 2 * BR), f32)],
        compiler_params=_params(("arbitrary",)),
        name="hy_filter",
    )(emb, w1p, b1, w2, b2, freq, w3d, deltas2)


def _seq_chunk(ref, i, nchunks, w, b):
    r0 = pl.multiple_of(i * FFT_N2, FFT_N2)
    x = ref[pl.ds(r0, FFT_N2), :].astype(f32)
    if w is None:
        return x
    total = nchunks * FFT_N2
    p0 = pl.multiple_of(jnp.maximum(r0 - HALO, 0), HALO)
    n0 = pl.multiple_of(jnp.minimum(r0 + FFT_N2, total - HALO), HALO)
    prev = ref[pl.ds(p0, HALO), :].astype(f32)
    nxt = ref[pl.ds(n0, HALO), :].astype(f32)
    return _conv3(x, prev, nxt, i > 0, i < nchunks - 1, w, b)


def _dft_outer_kernel(K, k1t, conv, *refs):
    if conv:
        x_ref, cw_ref, cb_ref, f_ref, o_ref, xs_ref = refs
        w, b = cw_ref[0], cb_ref[0]
    else:
        x_ref, f_ref, o_ref, xs_ref = refs
        w = b = None

    @pl.when(pl.program_id(2) == 0)
    def _():
        def fill(i, carry):
            xs_ref[pl.ds(pl.multiple_of(i * PITCH_X, 8), FFT_N2), :] = _seq_chunk(x_ref, i, K, w, b)
            return carry
        lax.fori_loop(0, K, fill, 0)

    f = f_ref[0]

    def body(g, carry):
        j0 = g * FFT_NJ
        x = jnp.concatenate([xs_ref[pl.ds(j0 + i, K, stride=PITCH_X), :] for i in range(FFT_NJ)], axis=1)
        r = jnp.dot(f, x.astype(bf16), preferred_element_type=f32)
        for i in range(FFT_NJ):
            sl = slice(i * 128, (i + 1) * 128)
            o_ref[pl.ds(j0 + i, k1t, stride=PITCH_A), :] = r[:k1t, sl]
            o_ref[pl.ds(FFT_N2 + j0 + i, k1t, stride=PITCH_A), :] = r[k1t:, sl]
        return carry

    lax.fori_loop(0, FFT_N2 // FFT_NJ, body, 0, unroll=2)
    zero = jnp.zeros((k1t, 128), f32)
    for i in range(PITCH_A - 2 * FFT_N2):
        o_ref[pl.ds(2 * FFT_N2 + i, k1t, stride=PITCH_A), :] = zero


def _dft_outer(x, col0, ncol, fmat, B, K, K1p, k1t, conv=None):
    nt = K1p // k1t
    in_specs = [pl.BlockSpec((K * FFT_N2, 128), lambda b, c, t: (b, col0 + c))]
    args = [x]
    if conv is not None:
        cw, cb, s = conv
        in_specs += [pl.BlockSpec((1, 3, 128), lambda b, c, t: (s, 0, c)),
                     pl.BlockSpec((1, 1, 128), lambda b, c, t: (s, 0, c))]
        args += [cw, cb]
    in_specs.append(pl.BlockSpec((1, 2 * k1t, K), lambda b, c, t: (t, 0, 0)))
    args.append(fmat)
    return pl.pallas_call(
        functools.partial(_dft_outer_kernel, K, k1t, conv is not None),
        grid=(B, ncol, nt),
        in_specs=in_specs,
        out_specs=pl.BlockSpec((k1t * PITCH_A, 128), lambda b, c, t: (b * nt + t, c)),
        out_shape=jax.ShapeDtypeStruct((B * K1p * PITCH_A, 128 * ncol), f32),
        scratch_shapes=[pltpu.VMEM((K * PITCH_X, 128), f32)],
        compiler_params=_params(("arbitrary", "arbitrary", "arbitrary")),
        name="dft_outer",
    )(*args)


def _dft_inner_kernel(conv, *refs):
    if conv:
        a_ref, gf_ref, h_ref, gi_ref, o_ref = refs
    else:
        a_ref, gf_ref, o_ref = refs
    n2 = FFT_N2
    dot = functools.partial(jnp.dot, preferred_element_type=f32)
    for kk in range(FFT_K1T):
        gf = gf_ref[kk]
        x = dot(gf[:, :n2], a_ref[0, kk, :n2].astype(bf16)) + dot(gf[:, n2:], a_ref[0, kk, n2:2 * n2].astype(bf16))
        if conv:
            xr, xi = x[:n2], x[n2:]
            hr, hi = h_ref[0, kk, :n2], h_ref[0, kk, n2:2 * n2]
            yr = (xr * hr - xi * hi).astype(bf16)
            yi = (xr * hi + xi * hr).astype(bf16)
            gi = gi_ref[kk]
            x = dot(gi[:, :n2], yr) + dot(gi[:, n2:], yi)
        o_ref[0, kk, :2 * n2] = x
        o_ref[0, kk, 2 * n2:] = jnp.zeros((PITCH_A - 2 * n2, BR), f32)


def _dft_inner(a4, gf, h4=None, gi=None, hblk=0):
    B, k1p, pa, W = a4.shape
    conv = h4 is not None
    kt = FFT_K1T
    r2 = 2 * FFT_N2
    in_specs = [pl.BlockSpec((1, kt, pa, BR), lambda k, w, b: (b, k, 0, w)),
                pl.BlockSpec((kt, r2, r2), lambda k, w, b: (k, 0, 0))]
    args = [a4, gf]
    if conv:
        in_specs += [pl.BlockSpec((1, kt, pa, BR), lambda k, w, b: (0, k, 0, hblk)),
                     pl.BlockSpec((kt, r2, r2), lambda k, w, b: (k, 0, 0))]
        args += [h4, gi]
    return pl.pallas_call(
        functools.partial(_dft_inner_kernel, conv),
        grid=(k1p // kt, W // BR, B),
        in_specs=in_specs,
        out_specs=pl.BlockSpec((1, kt, pa, BR), lambda k, w, b: (b, k, 0, w)),
        out_shape=jax.ShapeDtypeStruct((B, k1p, pa, W), f32),
        compiler_params=_params(("arbitrary", "arbitrary", "arbitrary")),
        name="dft_inner_conv" if conv else "dft_inner_spec",
    )(*args)


def _idft_outer_kernel(kh, K1p, final, conv_s, *refs):
    it = iter(refs)
    b_ref, fr_ref, fi_ref, mul_ref, s_ref, sc_ref, sk_ref, cwm_ref, cbm_ref = [next(it) for _ in range(9)]
    cws_ref, cbs_ref = (next(it), next(it)) if conv_s else (None, None)
    g_ref = next(it) if final else None
    o_ref, ys_ref = next(it), next(it)
    fr = fr_ref[...]
    fi = fi_ref[...]
    dot = functools.partial(jnp.dot, preferred_element_type=f32)

    def body(g, carry):
        j0 = g * FFT_NJ
        br = jnp.concatenate([b_ref[pl.ds(j0 + i, K1p, stride=PITCH_A), :] for i in range(FFT_NJ)], axis=1)
        bi = jnp.concatenate([b_ref[pl.ds(FFT_N2 + j0 + i, K1p, stride=PITCH_A), :] for i in range(FFT_NJ)], axis=1)
        y = dot(fr, br.astype(bf16)) + dot(fi, bi.astype(bf16))
        for i in range(FFT_NJ):
            ys_ref[pl.ds(j0 + i, kh, stride=PITCH_X), :] = y[:, i * 128:(i + 1) * 128]
        return carry

    lax.fori_loop(0, FFT_N2 // FFT_NJ, body, 0, unroll=2)
    sc = sc_ref[...]
    sk = sk_ref[...]

    def epi(i, carry):
        y = ys_ref[pl.ds(pl.multiple_of(i * PITCH_X, 8), FFT_N2), :]
        m = _seq_chunk(mul_ref, i, kh, cwm_ref[0], cbm_ref[0])
        s = _seq_chunk(s_ref, i, kh, cws_ref[0], cbs_ref[0]) if conv_s else _seq_chunk(s_ref, i, kh, None, None)
        z = m * (y * sc + s * sk)
        if final:
            z = z * _silu(_seq_chunk(g_ref, i, kh, None, None))
        o_ref[pl.ds(pl.multiple_of(i * FFT_N2, FFT_N2), FFT_N2), :] = z.astype(o_ref.dtype)
        return carry

    lax.fori_loop(0, kh, epi, 0)


def _idft_outer(b2, fr, fi, mul, s, scale, skip, conv_w, conv_b, B, kh, K1p, g=None):
    L = kh * FFT_N2
    once = pl.Buffered(1) if L > 4096 else None
    mul_a, mul_c, mul_s = mul
    s_a, s_c, s_s = s
    conv_s = s_s is not None
    vec = pl.BlockSpec((1, 128), lambda b, c: (0, c))
    in_specs = [pl.BlockSpec((K1p * PITCH_A, 128), lambda b, c: (b, c), pipeline_mode=once),
                pl.BlockSpec((kh, K1p), lambda b, c: (0, 0)), pl.BlockSpec((kh, K1p), lambda b, c: (0, 0)),
                pl.BlockSpec((L, 128), lambda b, c: (b, mul_c + c), pipeline_mode=once),
                pl.BlockSpec((L, 128), lambda b, c: (b, s_c + c), pipeline_mode=once),
                vec, vec,
                pl.BlockSpec((1, 3, 128), lambda b, c: (mul_s, 0, c)),
                pl.BlockSpec((1, 1, 128), lambda b, c: (mul_s, 0, c))]
    args = [b2, fr, fi, mul_a, s_a, scale, skip, conv_w, conv_b]
    if conv_s:
        in_specs += [pl.BlockSpec((1, 3, 128), lambda b, c: (s_s, 0, c)),
                     pl.BlockSpec((1, 1, 128), lambda b, c: (s_s, 0, c))]
        args += [conv_w, conv_b]
    if g is not None:
        g_a, g_c = g
        in_specs.append(pl.BlockSpec((L, 128), lambda b, c: (b, g_c + c), pipeline_mode=once))
        args.append(g_a)
    return pl.pallas_call(
        functools.partial(_idft_outer_kernel, kh, K1p, g is not None, conv_s),
        grid=(B, BR // 128),
        in_specs=in_specs,
        out_specs=pl.BlockSpec((L, 128), lambda b, c: (b, c)),
        out_shape=jax.ShapeDtypeStruct((B * L, BR), bf16),
        scratch_shapes=[pltpu.VMEM((kh * PITCH_X, 128), f32)],
        compiler_params=_params(("arbitrary", "arbitrary")),
        name="idft_outer",
    )(*args)


def _fft_tables(L):
    n = 2 * L
    n2 = FFT_N2
    n1 = n // n2
    kh = n1 // 2
    k1p = -(-(kh + 1) // 16) * 16
    k1t = k1p if k1p <= 48 else 48
    nt = k1p // k1t
    two_pi = 2.0 * math.pi
    k1 = jnp.arange(k1p, dtype=jnp.int32)

    def outer(K):
        m = jnp.arange(K, dtype=jnp.int32)
        ang = (two_pi / n1) * ((k1[:, None] * m[None, :]) % n1).astype(f32)
        re = jnp.cos(ang).reshape(nt, k1t, K)
        im = (-jnp.sin(ang)).reshape(nt, k1t, K)
        return jnp.concatenate([re, im], axis=1).astype(bf16)

    r1 = jnp.arange(kh, dtype=jnp.int32)
    wgt = jnp.where((k1 == 0) | (k1 == kh), 1.0, jnp.where(k1 < kh, 2.0, 0.0)).astype(f32)
    angi = (two_pi / n1) * ((r1[:, None] * k1[None, :]) % n1).astype(f32)
    fi_re = (jnp.cos(angi) * wgt[None, :]).astype(bf16)
    fi_im = (-jnp.sin(angi) * wgt[None, :]).astype(bf16)
    kk = k1[:, None, None]
    a = jnp.arange(n2, dtype=jnp.int32)[None, :, None]
    c = jnp.arange(n2, dtype=jnp.int32)[None, None, :]
    angp = (two_pi / n) * ((a * c * n1 + c * kk) % n).astype(f32)
    mr, mi = jnp.cos(angp), -jnp.sin(angp)
    g_fwd = jnp.concatenate([jnp.concatenate([mr, -mi], axis=2),
                             jnp.concatenate([mi, mr], axis=2)], axis=1).astype(bf16)
    angq = (two_pi / n) * ((a * c * n1 + a * kk) % n).astype(f32)
    vr, vi = jnp.cos(angq), jnp.sin(angq)
    g_inv = jnp.concatenate([jnp.concatenate([vr, -vi], axis=2),
                             jnp.concatenate([vi, vr], axis=2)], axis=1).astype(bf16)
    return dict(n=n, n1=n1, kh=kh, k1p=k1p, k1t=k1t, f_half=outer(kh), f_full=outer(n1),
                fi_re=fi_re, fi_im=fi_im, g_fwd=g_fwd, g_inv=g_inv)


def _hy_embedding(L):
    n = jnp.arange(2 * L, dtype=jnp.int32)
    pos = jnp.where(n < L, n, 2 * L - n).astype(f32)
    t = pos / (L - 1)
    f = jnp.linspace(1e-4, HY_BANDS - 1, HY_BANDS, dtype=f32)
    ang = (2.0 * math.pi / L) * pos[:, None] * f[None, :]
    valid = (n != L).astype(f32)
    pad = jnp.zeros((2 * L, HY_EMB_PAD - HY_EMB - 1), f32)
    return jnp.concatenate([t[:, None], jnp.cos(ang), -jnp.sin(ang), valid[:, None], pad], axis=-1)


def _hy_branch(proj, lw, emb, tabs, B, L):
    n1, kh, k1p, k1t = tabs["n1"], tabs["kh"], tabs["k1p"], tabs["k1t"]
    cw, cb = lw["hy_conv_w"], lw["hy_conv_b"]
    taps, scale = _hy_filter(emb, lw["hy_w1"], lw["hy_b1"], lw["hy_w2"], lw["hy_b2"], lw["hy_freq"],
                             lw["hy_w3"], lw["hy_deltas"], L)
    ta = _dft_outer(taps, 0, 2 * BR // 128, tabs["f_full"], 1, n1, k1p, k1t)
    h4 = _dft_inner(ta.reshape(1, k1p, PITCH_A, 2 * BR), tabs["g_fwd"])
    scale = scale * (1.0 / tabs["n"])
    col = lambda s: (BLK_HY + s) * (BR // 128)

    def conv(x, col0, order, stream):
        a = _dft_outer(x, col0, BR // 128, tabs["f_half"], B, kh, k1p, k1t,
                       None if stream is None else (cw, cb, stream))
        b4 = _dft_inner(a.reshape(B, k1p, PITCH_A, BR), tabs["g_fwd"], h4, tabs["g_inv"], order)
        return b4.reshape(B * k1p * PITCH_A, BR)

    def inv(b2, order, mul, s, g=None):
        return _idft_outer(b2, tabs["fi_re"], tabs["fi_im"], mul, s, scale[:, order * BR:(order + 1) * BR],
                           lw["hy_skip"][order:order + 1], cw, cb, B, kh, k1p, g)

    z = inv(conv(proj, col(0), 0, 0), 0, (proj, col(1), 1), (proj, col(0), 0))
    return inv(conv(z, 0, 1, None), 1, (proj, col(2), 2), (z, 0, None), (proj, col(3)))


def _rotary_tables(L):
    inv = ROPE_BASE ** (-jnp.arange(0, DH, 2, dtype=f32) / DH)
    ang = jnp.arange(L, dtype=f32)[:, None] * inv[None, :]
    cos, sin = jnp.cos(ang), jnp.sin(ang)
    return jnp.concatenate([cos, cos], axis=-1), jnp.concatenate([-sin, sin], axis=-1)


def _block_diag(w):
    n, e, fo = w.shape
    eye = jnp.eye(n, dtype=w.dtype)
    return (eye[:, None, :, None] * w[:, :, None, :]).reshape(n * e, n * fo)


def _permute_w_in(w_in):
    ref_gla_small = 2048 + 1536
    ref_hy = ref_gla_small + 2 * GLA_RANK
    ref_ml = ref_hy + 2048
    ref_ml_small = ref_ml + 1536
    main = jnp.concatenate([w_in[..., :ref_gla_small], w_in[..., ref_hy:ref_ml], w_in[..., ref_ml:ref_ml_small]],
                           axis=-1)
    small = jnp.concatenate([w_in[..., ref_gla_small:ref_hy], w_in[..., ref_ml_small:]], axis=-1)
    small = jnp.pad(small, [(0, 0)] * (small.ndim - 1) + [(0, N_SMALL - small.shape[-1])])
    return main.astype(bf16), small.astype(bf16)


def _layer_weights(l, p):
    lw = {}
    lw["gla_wa"] = jnp.stack([
        jnp.pad(p["gla_w_a"][l, d], ((LANE_GLA_A + GLA_RANK * d, 128 - LANE_GLA_A - GLA_RANK * (d + 1)), (0, 0)))
        for d in range(2)]).astype(bf16)
    lw["gla_ba"] = p["gla_b_a"][l]
    lw["gla_ng"] = p["gla_norm_gain"][l].reshape(1, DH)
    lw["hy_conv_w"] = p["hy_conv_w"][l].reshape(3, 3, BR).transpose(1, 0, 2)
    lw["hy_conv_b"] = p["hy_conv_b"][l].reshape(3, 1, BR)
    lw["hy_w1"] = jnp.pad(p["hy_w1"][l], ((0, HY_EMB_PAD - HY_EMB), (0, 0)))
    lw["hy_b1"] = p["hy_b1"][l].reshape(1, HY_HIDDEN)
    lw["hy_w2"] = p["hy_w2"][l]
    lw["hy_b2"] = p["hy_b2"][l].reshape(1, HY_HIDDEN)
    lw["hy_freq"] = p["hy_freq"][l].reshape(1, HY_HIDDEN)
    lw["hy_w3"] = p["hy_w3"][l].reshape(HY_HIDDEN, 2, 2, BR).transpose(2, 0, 1, 3).reshape(2, HY_HIDDEN, 2 * BR)
    deltas = jnp.abs(jnp.linspace(HY_MIN_DECAY, HY_MAX_DECAY, BR, dtype=f32))
    lw["hy_deltas"] = jnp.concatenate([deltas, deltas]).reshape(1, 2 * BR)
    lw["hy_skip"] = p["hy_skip"][l]
    lw["ml_conv_w"] = p["ml_conv_w"][l]
    lw["ml_conv_b"] = p["ml_conv_b"][l].reshape(1, BR)
    lw["ml_wq"] = _block_diag(p["ml_wq"][l]).astype(bf16)
    lw["ml_wk"] = _block_diag(p["ml_wk"][l]).astype(bf16)
    lw["ml_wv"] = _block_diag(p["ml_wv"][l]).astype(bf16)
    bias = jnp.concatenate([p["ml_b_i"][l].reshape(-1), p["ml_b_f"][l].reshape(-1)])
    lw["ml_bias"] = jnp.pad(bias, (LANE_ML_I, 128 - LANE_ML_I - bias.shape[0])).reshape(1, 128)
    lw["ml_ng"] = p["ml_norm_gain"][l].reshape(1, BR)
    lw["ml_skip"] = p["ml_skip"][l].reshape(1, BR)
    return lw


def _mix(proj, small, lw, consts, B, L):
    cosf, sins, emb, tabs = consts
    proj3 = proj.reshape(B, L, N_PROJ)
    small3 = small.reshape(B, L, N_SMALL)
    return (
        _ret_branch(proj3, cosf, sins, B, L).reshape(B * L, BR),
        _gla_branch(proj3, small3, lw["gla_wa"], lw["gla_ba"], lw["gla_ng"], B, L).reshape(B * L, BR),
        _hy_branch(proj, lw, emb, tabs, B, L),
        _ml_branch(proj3, small3, lw["ml_conv_w"], lw["ml_conv_b"], lw["ml_wq"], lw["ml_wk"], lw["ml_wv"],
                   lw["ml_bias"], lw["ml_ng"], lw["ml_skip"], B, L).reshape(B * L, BR),
    )


def kernel(x_prompt, x_sample, c_prompt, c_sample, norm_gain, w_ada, b_ada, w_in, gla_w_a, gla_b_a, gla_norm_gain, hy_conv_w, hy_conv_b, hy_w1, hy_b1, hy_w2, hy_b2, hy_freq, hy_w3, hy_skip, ml_conv_w, ml_conv_b, ml_wq, ml_wk, ml_wv, ml_b_i, ml_b_f, ml_norm_gain, ml_skip, w_branch, w_gate, b_gate, w_out, final_gain):
    p = dict(gla_w_a=gla_w_a, gla_b_a=gla_b_a, gla_norm_gain=gla_norm_gain, hy_conv_w=hy_conv_w,
             hy_conv_b=hy_conv_b, hy_w1=hy_w1, hy_b1=hy_b1, hy_w2=hy_w2, hy_b2=hy_b2, hy_freq=hy_freq,
             hy_w3=hy_w3, hy_skip=hy_skip, ml_conv_w=ml_conv_w, ml_conv_b=ml_conv_b, ml_wq=ml_wq,
             ml_wk=ml_wk, ml_wv=ml_wv, ml_b_i=ml_b_i, ml_b_f=ml_b_f, ml_norm_gain=ml_norm_gain,
             ml_skip=ml_skip)
    groups = []
    for x, c in ((x_prompt, c_prompt), (x_sample, c_sample)):
        B, L, _ = x.shape
        groups.append(dict(B=B, L=L, x=x.reshape(B * L, D_MODEL),
                           consts=_rotary_tables(L) + (_hy_embedding(L), _fft_tables(L))))
    nb = [g["B"] for g in groups]
    rows = -(-sum(nb) // 8) * 8
    c_all = jnp.concatenate([c_prompt, c_sample, jnp.zeros((rows - sum(nb), D_MODEL), f32)], axis=0)
    mod_all = _ada_all(c_all, w_ada, b_ada)

    w_in_p, w_in_s = _permute_w_in(w_in)
    w_in_p = w_in_p.reshape(DEPTH, D_MODEL, N_PROJ // TN_IN, TN_IN).transpose(0, 2, 1, 3)
    ng = D_MODEL // TN_GATES
    wg = w_gate.astype(bf16).reshape(DEPTH, 4, D_MODEL, ng, TN_GATES).transpose(0, 3, 1, 2, 4)
    wb = w_branch.astype(bf16).reshape(DEPTH, 4, BR, ng, TN_GATES).transpose(0, 3, 1, 2, 4)
    bg = b_gate.reshape(DEPTH, 4, 1, ng, TN_GATES).transpose(0, 3, 1, 2, 4)
    wo = w_out.astype(bf16)
    fg = final_gain.reshape(1, D_MODEL)
    for l in range(DEPTH):
        lw = _layer_weights(l, p)
        gain = norm_gain[l].reshape(1, D_MODEL)
        start = 0
        for g in groups:
            B, L = g["B"], g["L"]
            mod = mod_all[l, start:start + B].reshape(B, 3, D_MODEL)
            start += B
            proj, small, h = _inproj(g["x"], mod, gain, w_in_p[l], w_in_s[l], L)
            branches = _mix(proj, small, lw, g["consts"], B, L)
            merged = _gated_merge(h, branches, wg[l], bg[l], wb[l], L)
            g["x"] = _outproj(merged, wo[l], g["x"], mod, fg, L, l == DEPTH - 1)
    return tuple(g["x"].reshape(g["B"], g["L"], D_MODEL) for g in groups)
```

```python
import functools
import math

import jax
import jax.numpy as jnp
from jax import lax
from jax.experimental import pallas as pl
from jax.experimental.pallas import tpu as pltpu

f32 = jnp.float32
bf16 = jnp.bfloat16

D_MODEL = 2048
DEPTH = 4
BR = 512
HEADS = 4
DH = 128
ROPE_BASE = 10000.0
GLA_DK = 64
GLA_KW = 256
GLA_RANK = 16
GLA_GATE_NORM = 16.0
HY_EMB = 33
HY_EMB_PAD = 40
HY_BANDS = 16
HY_HIDDEN = 64
HY_MIN_DECAY = math.log(1e-2) / 1.5
HY_MAX_DECAY = math.log(1e-2) / 0.3
ML_BLOCK = 4
EPS = 1e-6
NEG = -1e30

BLK_RET = 0
BLK_GLA_QK = 4
BLK_GLA_V = 5
BLK_GLA_G = 6
BLK_HY = 7
BLK_ML = 11
N_PROJ = 7168
N_SMALL = 128
LANE_GLA_A = 0
LANE_ML_I = 32
LANE_ML_F = 40

CHUNK = 128
HALO = 16
FFT_N2 = 128
FFT_K1T = 8
FFT_NJ = 4
PITCH_X = FFT_N2 + 8
PITCH_A = 2 * FFT_N2 + 8
TM_IN = 1024
TN_IN = 896
TM_GATES = 1024
TN_GATES = 512
TM_OUT = 512
SEQ_PER_STEP = 2
VMEM_LIMIT = 60 * 1024 * 1024

LOG_GAMMA = tuple(math.log(1.0 - 2.0 ** (-5.0 - h)) for h in range(HEADS))


def _params(sem):
    return pltpu.CompilerParams(dimension_semantics=sem, vmem_limit_bytes=VMEM_LIMIT)


def _bdot(a, b):
    return jnp.dot(a.astype(bf16), b.astype(bf16), preferred_element_type=f32)


def _bdot_nt(a, b):
    return lax.dot_general(a.astype(bf16), b.astype(bf16), (((1,), (1,)), ((), ())),
                           preferred_element_type=f32)


def _bdot_tn(a, b):
    return jnp.dot(a.T.astype(bf16), b.astype(bf16), preferred_element_type=f32)


def _dot3(a, b):
    a1 = a.astype(bf16)
    ra = a - a1.astype(f32)
    a2 = ra.astype(bf16)
    a3 = (ra - a2.astype(f32)).astype(bf16)
    b1 = b.astype(bf16)
    rb = b - b1.astype(f32)
    b2 = rb.astype(bf16)
    b3 = (rb - b2.astype(f32)).astype(bf16)
    d = functools.partial(jnp.dot, preferred_element_type=f32)
    return (d(a1, b1) + (d(a1, b2) + d(a2, b1))) + ((d(a2, b2) + d(a1, b3)) + d(a3, b1))


def _cumsum01(tri, x):
    x1 = x.astype(bf16)
    r = x - x1.astype(f32)
    x2 = r.astype(bf16)
    x3 = (r - x2.astype(f32)).astype(bf16)
    d = functools.partial(jnp.dot, preferred_element_type=f32)
    return d(tri, x1) + (d(tri, x2) + d(tri, x3))


def _sigmoid(x):
    return jax.nn.sigmoid(x)


def _silu(x):
    return x * jax.nn.sigmoid(x)


def _log_sigmoid(x):
    return jnp.minimum(x, 0.0) - jnp.log1p(jnp.exp(-jnp.abs(x)))


def _tri(n, rev):
    ii = lax.broadcasted_iota(jnp.int32, (n, n), 0)
    jj = lax.broadcasted_iota(jnp.int32, (n, n), 1)
    return (jj >= ii) if rev else (jj <= ii)


def _prev_rows(x, first):
    row = lax.broadcasted_iota(jnp.int32, x.shape, 0)
    return jnp.where(row == 0, first, pltpu.roll(x, 1, 0))


def _next_rows(x, last):
    n = x.shape[0]
    row = lax.broadcasted_iota(jnp.int32, x.shape, 0)
    return jnp.where(row == n - 1, last, pltpu.roll(x, n - 1, 0))


def _conv3(x, prev, nxt, has_prev, has_next, w, b):
    first = jnp.where(has_prev, prev[HALO - 1:HALO, :], 0.0)
    last = jnp.where(has_next, nxt[0:1, :], 0.0)
    return _prev_rows(x, first) * w[0:1] + x * w[1:2] + _next_rows(x, last) * w[2:3] + b


def _head_norm(x, center):
    if center:
        x = x - jnp.mean(x, axis=-1, keepdims=True)
    return x * lax.rsqrt(jnp.mean(x * x, axis=-1, keepdims=True) + EPS)


def _ada_kernel(c_ref, w_ref, b_ref, o_ref):
    o_ref[0] = _bdot(_silu(c_ref[...]), w_ref[0]) + b_ref[0]


def _ada_all(c_all, w_ada, b_ada):
    rows = c_all.shape[0]
    tn = 1536
    return pl.pallas_call(
        _ada_kernel,
        grid=(DEPTH, 3 * D_MODEL // tn),
        in_specs=[
            pl.BlockSpec((rows, D_MODEL), lambda l, j: (0, 0)),
            pl.BlockSpec((1, D_MODEL, tn), lambda l, j: (l, 0, j)),
            pl.BlockSpec((1, 1, tn), lambda l, j: (l, 0, j)),
        ],
        out_specs=pl.BlockSpec((1, rows, tn), lambda l, j: (l, 0, j)),
        out_shape=jax.ShapeDtypeStruct((DEPTH, rows, 3 * D_MODEL), f32),
        compiler_params=_params(("arbitrary", "arbitrary")),
        name="ada",
    )(c_all, w_ada, b_ada.reshape(DEPTH, 1, 3 * D_MODEL))


ROWS_NORM = 256


def _modulated_norm(x, mod, gain):
    y = x * lax.rsqrt(jnp.mean(x * x, axis=-1, keepdims=True) + EPS) * gain
    return y * (1.0 + mod[1:2]) + mod[0:1]


def _norm_to_scratch(x_ref, mod_ref, gain_ref, h_ref):
    mod = mod_ref[0]
    gain = gain_ref[...]

    def step(i, carry):
        rows = pl.ds(pl.multiple_of(i * ROWS_NORM, ROWS_NORM), ROWS_NORM)
        h_ref[rows, :] = _modulated_norm(x_ref[rows, :], mod, gain).astype(bf16)
        return carry

    lax.fori_loop(0, x_ref.shape[0] // ROWS_NORM, step, 0)


def _inproj_kernel(x_ref, mod_ref, gain_ref, w_ref, ws_ref, o_ref, os_ref, h_ref):
    @pl.when(pl.program_id(1) == 0)
    def _():
        _norm_to_scratch(x_ref, mod_ref, gain_ref, h_ref)
        os_ref[...] = jnp.dot(h_ref[...], ws_ref[...], preferred_element_type=f32)

    o_ref[...] = jnp.dot(h_ref[...], w_ref[0], preferred_element_type=f32).astype(bf16)


def _inproj(x, mod, gain, w, ws, L):
    T = x.shape[0]
    tm = min(TM_IN, L)
    return pl.pallas_call(
        _inproj_kernel,
        grid=(T // tm, N_PROJ // TN_IN),
        in_specs=[
            pl.BlockSpec((tm, D_MODEL), lambda i, j: (i, 0)),
            pl.BlockSpec((1, 3, D_MODEL), lambda i, j: ((i * tm) // L, 0, 0)),
            pl.BlockSpec((1, D_MODEL), lambda i, j: (0, 0)),
            pl.BlockSpec((1, D_MODEL, TN_IN), lambda i, j: (j, 0, 0)),
            pl.BlockSpec((D_MODEL, N_SMALL), lambda i, j: (0, 0)),
        ],
        out_specs=[pl.BlockSpec((tm, TN_IN), lambda i, j: (i, j)),
                   pl.BlockSpec((tm, N_SMALL), lambda i, j: (i, 0)),
                   pl.BlockSpec((tm, D_MODEL), lambda i, j: (i, 0))],
        out_shape=[jax.ShapeDtypeStruct((T, N_PROJ), bf16), jax.ShapeDtypeStruct((T, N_SMALL), f32),
                   jax.ShapeDtypeStruct((T, D_MODEL), bf16)],
        compiler_params=_params(("arbitrary", "arbitrary")),
        name="inproj",
    )(x, mod, gain, w, ws)


def _gates_kernel(h_ref, b0_ref, b1_ref, b2_ref, b3_ref, wg_ref, bg_ref, wb_ref, o_ref):
    h = h_ref[...]
    merged = None
    for bi, br_ref in enumerate((b0_ref, b1_ref, b2_ref, b3_ref)):
        g = _sigmoid(jnp.dot(h, wg_ref[0, bi], preferred_element_type=f32) + bg_ref[0, bi])
        t = g * jnp.dot(br_ref[...], wb_ref[0, bi], preferred_element_type=f32)
        merged = t if merged is None else merged + t
    o_ref[...] = merged.astype(bf16)


def _gated_merge(h, branches, wg, bg, wb, L):
    T = h.shape[0]
    tm = min(TM_GATES, L)
    nj, _, _, tn = wg.shape
    br_spec = pl.BlockSpec((tm, BR), lambda i, j: (i, 0))
    return pl.pallas_call(
        _gates_kernel,
        grid=(T // tm, nj),
        in_specs=[
            pl.BlockSpec((tm, D_MODEL), lambda i, j: (i, 0)),
            br_spec, br_spec, br_spec, br_spec,
            pl.BlockSpec((1, 4, D_MODEL, tn), lambda i, j: (j, 0, 0, 0)),
            pl.BlockSpec((1, 4, 1, tn), lambda i, j: (j, 0, 0, 0)),
            pl.BlockSpec((1, 4, BR, tn), lambda i, j: (j, 0, 0, 0)),
        ],
        out_specs=pl.BlockSpec((tm, tn), lambda i, j: (i, j)),
        out_shape=jax.ShapeDtypeStruct((T, D_MODEL), bf16),
        compiler_params=_params(("arbitrary", "arbitrary")),
        name="gates",
    )(h, *branches, wg, bg, wb)


def _outproj_kernel(final, m_ref, wo_ref, x_ref, mod_ref, fg_ref, o_ref):
    y = x_ref[...] + mod_ref[0][2:3] * jnp.dot(m_ref[...], wo_ref[...], preferred_element_type=f32)
    if final:
        y = y * lax.rsqrt(jnp.mean(y * y, axis=-1, keepdims=True) + EPS) * fg_ref[...]
    o_ref[...] = y


def _outproj(merged, wo, x, mod, final_gain, L, final):
    T = x.shape[0]
    tm = min(TM_OUT, L)
    tile = pl.BlockSpec((tm, D_MODEL), lambda i: (i, 0))
    return pl.pallas_call(
        functools.partial(_outproj_kernel, final),
        grid=(T // tm,),
        in_specs=[tile,
                  pl.BlockSpec((D_MODEL, D_MODEL), lambda i: (0, 0), pipeline_mode=pl.Buffered(1)),
                  tile,
                  pl.BlockSpec((1, 3, D_MODEL), lambda i: ((i * tm) // L, 0, 0)),
                  pl.BlockSpec((1, D_MODEL), lambda i: (0, 0))],
        out_specs=tile,
        out_shape=jax.ShapeDtypeStruct((T, D_MODEL), f32),
        compiler_params=_params(("arbitrary",)),
        name="outproj",
    )(merged, wo, x, mod, final_gain)


def _chunk_idx(nc, rev):
    if rev:
        return lambda c: nc - 1 - c
    return lambda c: c


def _tok_spec(ns, nc, rev, blk, width=BR):
    idx = _chunk_idx(nc, rev)
    return pl.BlockSpec((ns, CHUNK, width), lambda b, c: (b, idx(c), blk))


def _per_sequence(body, batched, *refs):
    ns = [r for r, f in zip(refs, batched) if f][0].shape[0]
    for s in range(ns):
        body(*[r.at[s] if f else r for r, f in zip(refs, batched)])


def _seqs_per_step(B):
    return SEQ_PER_STEP if B % SEQ_PER_STEP == 0 else 1


def _pos_spec(nc, rev):
    if rev:
        return pl.BlockSpec((CHUNK, DH), lambda b, c: (nc - 1 - c, 0))
    return pl.BlockSpec((CHUNK, DH), lambda b, c: (c, 0))


def _const_spec(shape):
    nd = len(shape)
    return pl.BlockSpec(shape, lambda b, c: (0,) * nd)


def _ret_kernel(rev, final, batched, *refs):
    _per_sequence(functools.partial(_ret_body, rev, final), batched, *refs)


def _ret_body(rev, final, *refs):
    if final:
        q_ref, k_ref, v_ref, cos_ref, sin_ref, g_ref, rb_ref, o_ref, s_ref = refs
    else:
        q_ref, k_ref, v_ref, cos_ref, sin_ref, o_ref, s_ref = refs
    C = CHUNK

    @pl.when(pl.program_id(1) == 0)
    def _():
        s_ref[...] = jnp.zeros_like(s_ref)

    cos = cos_ref[...]
    sin = sin_ref[...]
    ii = lax.broadcasted_iota(jnp.int32, (C, C), 0)
    jj = lax.broadcasted_iota(jnp.int32, (C, C), 1)
    col = lax.broadcasted_iota(jnp.int32, (C, 1), 0).astype(f32)
    dist = (ii - jj).astype(f32)
    for h in range(HEADS):
        lg = LOG_GAMMA[h]
        sl = slice(h * DH, (h + 1) * DH)
        q = q_ref[:, sl].astype(f32)
        k = k_ref[:, sl].astype(f32)
        v = v_ref[:, sl]
        q = q * cos + pltpu.roll(q, DH // 2, 1) * sin
        k = (k * cos + pltpu.roll(k, DH // 2, 1) * sin) * (DH ** -0.5)
        if rev:
            gam = jnp.where(jj > ii, jnp.exp(-dist * lg), 0.0)
            qd = jnp.exp((C - col) * lg)
            kd = jnp.exp(col * lg)
        else:
            gam = jnp.where(jj <= ii, jnp.exp(dist * lg), 0.0)
            qd = jnp.exp((col + 1.0) * lg)
            kd = jnp.exp((C - 1.0 - col) * lg)
        s_prev = s_ref[h]
        o = _bdot(_bdot_nt(q, k) * gam, v) + qd * _bdot(q, s_prev)
        s_ref[h] = math.exp(C * lg) * s_prev + _bdot_tn(k * kd, v)
        if final:
            r = _head_norm(o + rb_ref[:, sl], center=True)
            o_ref[:, sl] = (r * _silu(g_ref[:, sl].astype(f32))).astype(o_ref.dtype)
        else:
            o_ref[:, sl] = o


def _ret_branch(proj, cosf, sins, B, L):
    nc = L // CHUNK
    ns = _seqs_per_step(B)

    def call(rev, final, extra):
        in_specs = [_tok_spec(ns, nc, rev, BLK_RET + i) for i in range(3)]
        in_specs += [_pos_spec(nc, rev), _pos_spec(nc, rev)]
        args = [proj, proj, proj, cosf, sins]
        batched = [True, True, True, False, False]
        if final:
            in_specs += [_tok_spec(ns, nc, rev, BLK_RET + 3), _tok_spec(ns, nc, rev, 0)]
            args += [proj, extra]
            batched += [True, True]
        batched += [True, True]
        return pl.pallas_call(
            functools.partial(_ret_kernel, rev, final, tuple(batched)),
            grid=(B // ns, nc),
            in_specs=in_specs,
            out_specs=_tok_spec(ns, nc, rev, 0),
            out_shape=jax.ShapeDtypeStruct((B, L, BR), bf16 if final else f32),
            scratch_shapes=[pltpu.VMEM((ns, HEADS, DH, DH), f32)],
            compiler_params=_params(("arbitrary", "arbitrary")),
            name="ret_fwd" if final else "ret_bwd",
        )(*args)

    rb = call(True, False, None)
    return call(False, True, rb)


def _gla_kernel(rev, final, batched, *refs):
    _per_sequence(functools.partial(_gla_body, rev, final), batched, *refs)


def _gla_body(rev, final, *refs):
    if final:
        qk_ref, v_ref, small_ref, wa_ref, ba_ref, g_ref, ob_ref, ng_ref, o_ref, st_ref = refs
    else:
        qk_ref, v_ref, small_ref, wa_ref, ba_ref, o_ref, st_ref = refs
    C = CHUNK

    @pl.when(pl.program_id(1) == 0)
    def _():
        st_ref[...] = jnp.zeros_like(st_ref)

    q = qk_ref[:, :GLA_KW].astype(f32) * (GLA_DK ** -0.5)
    k = qk_ref[:, GLA_KW:].astype(f32)
    v = v_ref[...].astype(f32)
    tri = _tri(C, rev)
    log_a = _log_sigmoid(_bdot(small_ref[...], wa_ref[...]) + ba_ref[...]) / GLA_GATE_NORM
    b = _cumsum01(tri.astype(bf16), log_a)
    ref_row = b[C // 2:C // 2 + 1, :]
    b_end = b[0:1, :] if rev else b[C - 1:C, :]
    qt = q * jnp.exp(b - ref_row)
    kt = k * jnp.exp(ref_row - b)
    inter = _bdot_nt(q * jnp.exp(b), st_ref[...])
    lane = lax.broadcasted_iota(jnp.int32, (1, GLA_KW), 1)
    for h in range(HEADS):
        sl = slice(h * DH, (h + 1) * DH)
        a = _bdot_nt(jnp.where(lane // GLA_DK == h, qt, 0.0), kt)
        o = _bdot(jnp.where(tri, a, 0.0), v[:, sl]) + inter[:, sl]
        if final:
            o = _head_norm(o + ob_ref[:, sl], center=False) * ng_ref[...]
            o_ref[:, sl] = (o * _silu(g_ref[:, sl].astype(f32))).astype(o_ref.dtype)
        else:
            o_ref[:, sl] = o
    ee = lax.broadcasted_iota(jnp.int32, (BR, GLA_KW), 0) // DH
    dd = lax.broadcasted_iota(jnp.int32, (BR, GLA_KW), 1) // GLA_DK
    upd = _bdot_tn(v, k * jnp.exp(b_end - b))
    st_ref[...] = st_ref[...] * jnp.exp(b_end) + jnp.where(ee == dd, upd, 0.0)


def _gla_branch(proj, small, wa_pad, ba, norm_gain, B, L):
    nc = L // CHUNK
    ns = _seqs_per_step(B)

    def call(rev, final, extra):
        d = 1 if rev else 0
        in_specs = [_tok_spec(ns, nc, rev, BLK_GLA_QK), _tok_spec(ns, nc, rev, BLK_GLA_V),
                    _tok_spec(ns, nc, rev, 0, N_SMALL),
                    _const_spec((128, GLA_KW)), _const_spec((1, GLA_KW))]
        args = [proj, proj, small, wa_pad[d], ba[d:d + 1]]
        batched = [True, True, True, False, False]
        if final:
            in_specs += [_tok_spec(ns, nc, rev, BLK_GLA_G), _tok_spec(ns, nc, rev, 0), _const_spec((1, DH))]
            args += [proj, extra, norm_gain]
            batched += [True, True, False]
        batched += [True, True]
        return pl.pallas_call(
            functools.partial(_gla_kernel, rev, final, tuple(batched)),
            grid=(B // ns, nc),
            in_specs=in_specs,
            out_specs=_tok_spec(ns, nc, rev, 0),
            out_shape=jax.ShapeDtypeStruct((B, L, BR), bf16 if final else f32),
            scratch_shapes=[pltpu.VMEM((ns, BR, GLA_KW), f32)],
            compiler_params=_params(("arbitrary", "arbitrary")),
            name="gla_fwd" if final else "gla_bwd",
        )(*args)

    ob = call(True, False, None)
    return call(False, True, ob)


def _ml_kernel(rev, final, batched, *refs):
    _per_sequence(functools.partial(_ml_body, rev, final), batched, *refs)


def _ml_body(rev, final, *refs):
    if final:
        (u_ref, up_ref, un_ref, small_ref, cw_ref, cb_ref, wq_ref, wk_ref, wv_ref, bias_ref,
         zg_ref, op_ref, hb_ref, ng_ref, skip_ref, o_ref, s_ref, n_ref, m_ref) = refs
    else:
        (u_ref, up_ref, un_ref, small_ref, cw_ref, cb_ref, wq_ref, wk_ref, wv_ref, bias_ref,
         o_ref, s_ref, n_ref, m_ref) = refs
    C = CHUNK
    c = pl.program_id(1)
    nc = pl.num_programs(1)

    @pl.when(c == 0)
    def _():
        s_ref[...] = jnp.zeros_like(s_ref)
        n_ref[...] = jnp.zeros_like(n_ref)
        m_ref[...] = jnp.full(m_ref.shape, NEG, f32)

    pos = (nc - 1 - c) if rev else c
    u = u_ref[...].astype(f32)
    xc = _silu(_conv3(u, up_ref[...].astype(f32), un_ref[...].astype(f32), pos > 0, pos < nc - 1,
                      cw_ref[...], cb_ref[...]))
    q = _bdot(xc, wq_ref[...])
    k = _bdot(xc, wk_ref[...]) * (DH ** -0.5)
    v = _bdot(u, wv_ref[...])

    gates = small_ref[...] + bias_ref[...]
    log_f = _log_sigmoid(gates)
    tri = _tri(C, rev)
    cum = _cumsum01(tri.astype(bf16), log_f)
    gates_t = gates.T
    cum_t = cum.T
    d = 1 if rev else 0
    for h in range(HEADS):
        sl = slice(h * DH, (h + 1) * DH)
        li = LANE_ML_I + 4 * d + h
        lf = LANE_ML_F + 4 * d + h
        b_col = cum[:, lf:lf + 1]
        b_row = cum_t[lf:lf + 1, :]
        i_col = gates[:, li:li + 1]
        i_row = gates_t[li:li + 1, :]
        b_end = b_col[0:1, :] if rev else b_col[C - 1:C, :]
        m_prev = m_ref[h][0:1, 0:1]
        s_prev = s_ref[h]
        n_prev = n_ref[h][0:1, :]
        qh = q[:, sl]
        kh = k[:, sl]
        vh = v[:, sl]
        dmat = jnp.where(tri, b_col - b_row + i_row, NEG)
        m_in = b_col + m_prev
        m_t = jnp.maximum(m_in, jnp.max(dmat, axis=1, keepdims=True))
        p = jnp.exp(dmat - m_t) * _bdot_nt(qh, kh)
        a_in = jnp.exp(m_in - m_t)
        num = _bdot(p, vh) + a_in * _bdot(qh, s_prev)
        den = jnp.sum(p, axis=1, keepdims=True) + a_in * jnp.sum(qh * n_prev, axis=1, keepdims=True)
        hd = num / jnp.maximum(jnp.abs(den), jnp.exp(-m_t))
        g_col = b_end - b_col + i_col
        m_loc = jnp.max(g_col, axis=0, keepdims=True)
        kw = kh * jnp.exp(g_col - m_loc)
        m_new = jnp.maximum(b_end + m_prev, m_loc)
        a_old = jnp.exp(b_end + m_prev - m_new)
        a_new = jnp.exp(m_loc - m_new)
        s_ref[h] = a_old * s_prev + a_new * _bdot_tn(kw, vh)
        n_ref[h] = jnp.broadcast_to(a_old * n_prev + a_new * jnp.sum(kw, axis=0, keepdims=True), (8, DH))
        m_ref[h] = jnp.broadcast_to(m_new, (8, DH))
        if final:
            hh = _sigmoid(op_ref[:, sl].astype(f32)) * (hd + hb_ref[:, sl])
            hh = _head_norm(hh, center=True) * ng_ref[:, sl] + skip_ref[:, sl] * xc[:, sl]
            o_ref[:, sl] = (hh * _silu(zg_ref[:, sl].astype(f32))).astype(o_ref.dtype)
        else:
            o_ref[:, sl] = hd


def _ml_branch(proj, small, conv_w, conv_b, wq, wk, wv, bias_small, norm_gain, skip, B, L):
    nc = L // CHUNK
    ns = _seqs_per_step(B)
    per = CHUNK // HALO
    last = L // HALO - 1

    def call(rev, final, extra):
        idx = _chunk_idx(nc, rev)
        prev_spec = pl.BlockSpec((ns, HALO, BR), lambda b, c: (b, jnp.maximum(idx(c) * per - 1, 0), BLK_ML))
        next_spec = pl.BlockSpec((ns, HALO, BR), lambda b, c: (b, jnp.minimum((idx(c) + 1) * per, last), BLK_ML))
        in_specs = [_tok_spec(ns, nc, rev, BLK_ML), prev_spec, next_spec, _tok_spec(ns, nc, rev, 0, N_SMALL),
                    _const_spec((3, BR)), _const_spec((1, BR)),
                    _const_spec((BR, BR)), _const_spec((BR, BR)), _const_spec((BR, BR)),
                    _const_spec((1, 128))]
        args = [proj, proj, proj, small, conv_w, conv_b, wq, wk, wv, bias_small]
        batched = [True] * 4 + [False] * 6
        if final:
            in_specs += [_tok_spec(ns, nc, rev, BLK_ML + 1), _tok_spec(ns, nc, rev, BLK_ML + 2),
                         _tok_spec(ns, nc, rev, 0), _const_spec((1, BR)), _const_spec((1, BR))]
            args += [proj, proj, extra, norm_gain, skip]
            batched += [True, True, True, False, False]
        batched += [True] * 4
        return pl.pallas_call(
            functools.partial(_ml_kernel, rev, final, tuple(batched)),
            grid=(B // ns, nc),
            in_specs=in_specs,
            out_specs=_tok_spec(ns, nc, rev, 0),
            out_shape=jax.ShapeDtypeStruct((B, L, BR), bf16 if final else f32),
            scratch_shapes=[pltpu.VMEM((ns, HEADS, DH, DH), f32), pltpu.VMEM((ns, HEADS, 8, DH), f32),
                            pltpu.VMEM((ns, HEADS, 8, DH), f32)],
            compiler_params=_params(("arbitrary", "arbitrary")),
            name="ml_fwd" if final else "ml_bwd",
        )(*args)

    hb = call(True, False, None)
    return call(False, True, hb)


def _hy_filt_kernel(emb_ref, w1_ref, b1_ref, w2_ref, b2_ref, fr_ref, w3_ref, dl_ref, taps_ref, sc_ref):
    r = pl.program_id(0)

    @pl.when(r == 0)
    def _():
        sc_ref[...] = jnp.zeros_like(sc_ref)

    emb = emb_ref[...]
    fr = fr_ref[...]
    hmid = jnp.sin(fr * (_dot3(emb, w1_ref[...]) + b1_ref[...]))
    hmid = jnp.sin(fr * (_dot3(hmid, w2_ref[...]) + b2_ref[...]))
    hout = _dot3(hmid, w3_ref[0])
    t = emb[:, 0:1]
    valid = emb[:, HY_EMB:HY_EMB + 1]
    taps = hout * jnp.exp(-t * dl_ref[...]) * valid
    taps_ref[...] = taps.astype(taps_ref.dtype)
    sc_ref[...] += jnp.sum(taps * taps, axis=0, keepdims=True)

    @pl.when(r == pl.num_programs(0) - 1)
    def _():
        sc_ref[...] = lax.rsqrt(sc_ref[...] + EPS)


def _hy_filter(emb, w1p, b1, w2, b2, freq, w3d, deltas2, L):
    n = 2 * L
    tr = 512
    half = L // tr
    return pl.pallas_call(
        _hy_filt_kernel,
        grid=(n // tr,),
        in_specs=[
            pl.BlockSpec((tr, HY_EMB_PAD), lambda r: (r, 0)),
            pl.BlockSpec((HY_EMB_PAD, HY_HIDDEN), lambda r: (0, 0)),
            pl.BlockSpec((1, HY_HIDDEN), lambda r: (0, 0)),
            pl.BlockSpec((HY_HIDDEN, HY_HIDDEN), lambda r: (0, 0)),
            pl.BlockSpec((1, HY_HIDDEN), lambda r: (0, 0)),
            pl.BlockSpec((1, HY_HIDDEN), lambda r: (0, 0)),
            pl.BlockSpec((1, HY_HIDDEN, 2 * BR), lambda r: (r // half, 0, 0)),
            pl.BlockSpec((1, 2 * BR), lambda r: (0, 0)),
        ],
        out_specs=[pl.BlockSpec((tr, 2 * BR), lambda r: (r, 0)),
                   pl.BlockSpec((1, 2 * BR), lambda r: (0, 0))],
        out_shape=[jax.ShapeDtypeStruct((n, 2 * BR), bf16), jax.ShapeDtypeStruct((1, 2 * BR), f32)],
        compiler_params=_params(("arbitrary",)),
        name="hy_filter",
    )(emb, w1p, b1, w2, b2, freq, w3d, deltas2)


def _seq_chunk(ref, i, nchunks, w, b):
    r0 = pl.multiple_of(i * FFT_N2, FFT_N2)
    x = ref[pl.ds(r0, FFT_N2), :].astype(f32)
    if w is None:
        return x
    total = nchunks * FFT_N2
    p0 = pl.multiple_of(jnp.maximum(r0 - HALO, 0), HALO)
    n0 = pl.multiple_of(jnp.minimum(r0 + FFT_N2, total - HALO), HALO)
    prev = ref[pl.ds(p0, HALO), :].astype(f32)
    nxt = ref[pl.ds(n0, HALO), :].astype(f32)
    return _conv3(x, prev, nxt, i > 0, i < nchunks - 1, w, b)


def _dft_outer_kernel(K, k1t, conv, *refs):
    if conv:
        x_ref, cw_ref, cb_ref, f_ref, o_ref, xs_ref = refs
        w, b = cw_ref[0], cb_ref[0]
    else:
        x_ref, f_ref, o_ref, xs_ref = refs
        w = b = None

    @pl.when(pl.program_id(2) == 0)
    def _():
        def fill(i, carry):
            xs_ref[pl.ds(pl.multiple_of(i * PITCH_X, 8), FFT_N2), :] = _seq_chunk(x_ref, i, K, w, b)
            return carry
        lax.fori_loop(0, K, fill, 0)

    f = f_ref[0]

    def body(g, carry):
        j0 = g * FFT_NJ
        x = jnp.concatenate([xs_ref[pl.ds(j0 + i, K, stride=PITCH_X), :] for i in range(FFT_NJ)], axis=1)
        r = jnp.dot(f, x.astype(bf16), preferred_element_type=f32)
        for i in range(FFT_NJ):
            sl = slice(i * 128, (i + 1) * 128)
            o_ref[pl.ds(j0 + i, k1t, stride=PITCH_A), :] = r[:k1t, sl]
            o_ref[pl.ds(FFT_N2 + j0 + i, k1t, stride=PITCH_A), :] = r[k1t:, sl]
        return carry

    lax.fori_loop(0, FFT_N2 // FFT_NJ, body, 0, unroll=2)
    zero = jnp.zeros((k1t, 128), f32)
    for i in range(PITCH_A - 2 * FFT_N2):
        o_ref[pl.ds(2 * FFT_N2 + i, k1t, stride=PITCH_A), :] = zero


def _dft_outer(x, col0, ncol, fmat, B, K, K1p, k1t, conv=None):
    nt = K1p // k1t
    in_specs = [pl.BlockSpec((K * FFT_N2, 128), lambda b, c, t: (b, col0 + c))]
    args = [x]
    if conv is not None:
        cw, cb, s = conv
        in_specs += [pl.BlockSpec((1, 3, 128), lambda b, c, t: (s, 0, c)),
                     pl.BlockSpec((1, 1, 128), lambda b, c, t: (s, 0, c))]
        args += [cw, cb]
    in_specs.append(pl.BlockSpec((1, 2 * k1t, K), lambda b, c, t: (t, 0, 0)))
    args.append(fmat)
    return pl.pallas_call(
        functools.partial(_dft_outer_kernel, K, k1t, conv is not None),
        grid=(B, ncol, nt),
        in_specs=in_specs,
        out_specs=pl.BlockSpec((k1t * PITCH_A, 128), lambda b, c, t: (b * nt + t, c)),
        out_shape=jax.ShapeDtypeStruct((B * K1p * PITCH_A, 128 * ncol), f32),
        scratch_shapes=[pltpu.VMEM((K * PITCH_X, 128), f32)],
        compiler_params=_params(("arbitrary", "arbitrary", "arbitrary")),
        name="dft_outer",
    )(*args)


def _dft_inner_kernel(conv, *refs):
    if conv:
        a_ref, gf_ref, h_ref, gi_ref, o_ref = refs
    else:
        a_ref, gf_ref, o_ref = refs
    n2 = FFT_N2
    dot = functools.partial(jnp.dot, preferred_element_type=f32)
    for kk in range(FFT_K1T):
        gf = gf_ref[kk]
        x = dot(gf[:, :n2], a_ref[0, kk, :n2].astype(bf16)) + dot(gf[:, n2:], a_ref[0, kk, n2:2 * n2].astype(bf16))
        if conv:
            xr, xi = x[:n2], x[n2:]
            hr, hi = h_ref[0, kk, :n2], h_ref[0, kk, n2:2 * n2]
            yr = (xr * hr - xi * hi).astype(bf16)
            yi = (xr * hi + xi * hr).astype(bf16)
            gi = gi_ref[kk]
            x = dot(gi[:, :n2], yr) + dot(gi[:, n2:], yi)
        o_ref[0, kk, :2 * n2] = x
        o_ref[0, kk, 2 * n2:] = jnp.zeros((PITCH_A - 2 * n2, BR), f32)


def _dft_inner(a4, gf, h4=None, gi=None, hblk=0):
    B, k1p, pa, W = a4.shape
    conv = h4 is not None
    kt = FFT_K1T
    r2 = 2 * FFT_N2
    in_specs = [pl.BlockSpec((1, kt, pa, BR), lambda k, w, b: (b, k, 0, w)),
                pl.BlockSpec((kt, r2, r2), lambda k, w, b: (k, 0, 0))]
    args = [a4, gf]
    if conv:
        in_specs += [pl.BlockSpec((1, kt, pa, BR), lambda k, w, b: (0, k, 0, hblk)),
                     pl.BlockSpec((kt, r2, r2), lambda k, w, b: (k, 0, 0))]
        args += [h4, gi]
    return pl.pallas_call(
        functools.partial(_dft_inner_kernel, conv),
        grid=(k1p // kt, W // BR, B),
        in_specs=in_specs,
        out_specs=pl.BlockSpec((1, kt, pa, BR), lambda k, w, b: (b, k, 0, w)),
        out_shape=jax.ShapeDtypeStruct((B, k1p, pa, W), f32),
        compiler_params=_params(("arbitrary", "arbitrary", "arbitrary")),
        name="dft_inner_conv" if conv else "dft_inner_spec",
    )(*args)


def _idft_outer_kernel(kh, K1p, final, conv_s, *refs):
    it = iter(refs)
    b_ref, fr_ref, fi_ref, mul_ref, s_ref, sc_ref, sk_ref, cwm_ref, cbm_ref = [next(it) for _ in range(9)]
    cws_ref, cbs_ref = (next(it), next(it)) if conv_s else (None, None)
    g_ref = next(it) if final else None
    o_ref, ys_ref = next(it), next(it)
    fr = fr_ref[...]
    fi = fi_ref[...]
    dot = functools.partial(jnp.dot, preferred_element_type=f32)

    def body(g, carry):
        j0 = g * FFT_NJ
        br = jnp.concatenate([b_ref[pl.ds(j0 + i, K1p, stride=PITCH_A), :] for i in range(FFT_NJ)], axis=1)
        bi = jnp.concatenate([b_ref[pl.ds(FFT_N2 + j0 + i, K1p, stride=PITCH_A), :] for i in range(FFT_NJ)], axis=1)
        pad = fr.shape[1] - K1p
        if pad:
            zero = jnp.zeros((pad, br.shape[1]), f32)
            br = jnp.concatenate([br, zero], axis=0)
            bi = jnp.concatenate([bi, zero], axis=0)
        y = dot(fr, br.astype(bf16)) + dot(fi, bi.astype(bf16))
        for i in range(FFT_NJ):
            ys_ref[pl.ds(j0 + i, kh, stride=PITCH_X), :] = y[:, i * 128:(i + 1) * 128]
        return carry

    lax.fori_loop(0, FFT_N2 // FFT_NJ, body, 0, unroll=2)
    sc = sc_ref[...]
    sk = sk_ref[...]

    def epi(i, carry):
        y = ys_ref[pl.ds(pl.multiple_of(i * PITCH_X, 8), FFT_N2), :]
        m = _seq_chunk(mul_ref, i, kh, cwm_ref[0], cbm_ref[0])
        s = _seq_chunk(s_ref, i, kh, cws_ref[0], cbs_ref[0]) if conv_s else _seq_chunk(s_ref, i, kh, None, None)
        z = m * (y * sc + s * sk)
        if final:
            z = z * _silu(_seq_chunk(g_ref, i, kh, None, None))
        o_ref[pl.ds(pl.multiple_of(i * FFT_N2, FFT_N2), FFT_N2), :] = z.astype(o_ref.dtype)
        return carry

    lax.fori_loop(0, kh, epi, 0)


def _idft_outer(b2, fr, fi, mul, s, scale, skip, conv_w, conv_b, B, kh, K1p, g=None):
    L = kh * FFT_N2
    once = pl.Buffered(1) if L > 4096 else None
    mul_a, mul_c, mul_s = mul
    s_a, s_c, s_s = s
    conv_s = s_s is not None
    vec = pl.BlockSpec((1, 128), lambda b, c: (0, c))
    in_specs = [pl.BlockSpec((K1p * PITCH_A, 128), lambda b, c: (b, c), pipeline_mode=once),
                pl.BlockSpec(fr.shape, lambda b, c: (0, 0)), pl.BlockSpec(fi.shape, lambda b, c: (0, 0)),
                pl.BlockSpec((L, 128), lambda b, c: (b, mul_c + c), pipeline_mode=once),
                pl.BlockSpec((L, 128), lambda b, c: (b, s_c + c), pipeline_mode=once),
                vec, vec,
                pl.BlockSpec((1, 3, 128), lambda b, c: (mul_s, 0, c)),
                pl.BlockSpec((1, 1, 128), lambda b, c: (mul_s, 0, c))]
    args = [b2, fr, fi, mul_a, s_a, scale, skip, conv_w, conv_b]
    if conv_s:
        in_specs += [pl.BlockSpec((1, 3, 128), lambda b, c: (s_s, 0, c)),
                     pl.BlockSpec((1, 1, 128), lambda b, c: (s_s, 0, c))]
        args += [conv_w, conv_b]
    if g is not None:
        g_a, g_c = g
        in_specs.append(pl.BlockSpec((L, 128), lambda b, c: (b, g_c + c), pipeline_mode=once))
        args.append(g_a)
    return pl.pallas_call(
        functools.partial(_idft_outer_kernel, kh, K1p, g is not None, conv_s),
        grid=(B, BR // 128),
        in_specs=in_specs,
        out_specs=pl.BlockSpec((L, 128), lambda b, c: (b, c)),
        out_shape=jax.ShapeDtypeStruct((B * L, BR), bf16),
        scratch_shapes=[pltpu.VMEM((kh * PITCH_X, 128), f32)],
        compiler_params=_params(("arbitrary", "arbitrary")),
        name="idft_outer",
    )(*args)


def _fft_tables(L):
    n = 2 * L
    n2 = FFT_N2
    n1 = n // n2
    kh = n1 // 2
    k1t = 48 if kh + 1 > 48 else -(-(kh + 1) // 8) * 8
    k1p = -(-(kh + 1) // k1t) * k1t
    nt = k1p // k1t
    two_pi = 2.0 * math.pi
    k1 = jnp.arange(k1p, dtype=jnp.int32)

    def outer(K):
        m = jnp.arange(K, dtype=jnp.int32)
        ang = (two_pi / n1) * ((k1[:, None] * m[None, :]) % n1).astype(f32)
        re = jnp.cos(ang).reshape(nt, k1t, K)
        im = (-jnp.sin(ang)).reshape(nt, k1t, K)
        return jnp.concatenate([re, im], axis=1).astype(bf16)

    r1 = jnp.arange(kh, dtype=jnp.int32)
    wgt = jnp.where((k1 == 0) | (k1 == kh), 1.0, jnp.where(k1 < kh, 2.0, 0.0)).astype(f32)
    angi = (two_pi / n1) * ((r1[:, None] * k1[None, :]) % n1).astype(f32)
    kpad = ((0, 0), (0, -k1p % 16))
    fi_re = jnp.pad(jnp.cos(angi) * wgt[None, :], kpad).astype(bf16)
    fi_im = jnp.pad(-jnp.sin(angi) * wgt[None, :], kpad).astype(bf16)
    kk = k1[:, None, None]
    a = jnp.arange(n2, dtype=jnp.int32)[None, :, None]
    c = jnp.arange(n2, dtype=jnp.int32)[None, None, :]
    angp = (two_pi / n) * ((a * c * n1 + c * kk) % n).astype(f32)
    mr, mi = jnp.cos(angp), -jnp.sin(angp)
    g_fwd = jnp.concatenate([jnp.concatenate([mr, -mi], axis=2),
                             jnp.concatenate([mi, mr], axis=2)], axis=1).astype(bf16)
    angq = (two_pi / n) * ((a * c * n1 + a * kk) % n).astype(f32)
    vr, vi = jnp.cos(angq), jnp.sin(angq)
    g_inv = jnp.concatenate([jnp.concatenate([vr, -vi], axis=2),
                             jnp.concatenate([vi, vr], axis=2)], axis=1).astype(bf16)
    return dict(n=n, n1=n1, kh=kh, k1p=k1p, k1t=k1t, f_half=outer(kh), f_full=outer(n1),
                fi_re=fi_re, fi_im=fi_im, g_fwd=g_fwd, g_inv=g_inv)


def _hy_embedding(L):
    n = jnp.arange(2 * L, dtype=jnp.int32)
    pos = jnp.where(n < L, n, 2 * L - n).astype(f32)
    t = pos / (L - 1)
    f = jnp.linspace(1e-4, HY_BANDS - 1, HY_BANDS, dtype=f32)
    ang = (2.0 * math.pi / L) * pos[:, None] * f[None, :]
    valid = (n != L).astype(f32)
    pad = jnp.zeros((2 * L, HY_EMB_PAD - HY_EMB - 1), f32)
    return jnp.concatenate([t[:, None], jnp.cos(ang), -jnp.sin(ang), valid[:, None], pad], axis=-1)


def _hy_branch(proj, lw, emb, tabs, B, L):
    n1, kh, k1p, k1t = tabs["n1"], tabs["kh"], tabs["k1p"], tabs["k1t"]
    cw, cb = lw["hy_conv_w"], lw["hy_conv_b"]
    taps, scale = _hy_filter(emb, lw["hy_w1"], lw["hy_b1"], lw["hy_w2"], lw["hy_b2"], lw["hy_freq"],
                             lw["hy_w3"], lw["hy_deltas"], L)
    ta = _dft_outer(taps, 0, 2 * BR // 128, tabs["f_full"], 1, n1, k1p, k1t)
    h4 = _dft_inner(ta.reshape(1, k1p, PITCH_A, 2 * BR), tabs["g_fwd"])
    scale = scale * (1.0 / tabs["n"])
    col = lambda s: (BLK_HY + s) * (BR // 128)

    def conv(x, col0, order, stream):
        a = _dft_outer(x, col0, BR // 128, tabs["f_half"], B, kh, k1p, k1t,
                       None if stream is None else (cw, cb, stream))
        b4 = _dft_inner(a.reshape(B, k1p, PITCH_A, BR), tabs["g_fwd"], h4, tabs["g_inv"], order)
        return b4.reshape(B * k1p * PITCH_A, BR)

    def inv(b2, order, mul, s, g=None):
        return _idft_outer(b2, tabs["fi_re"], tabs["fi_im"], mul, s, scale[:, order * BR:(order + 1) * BR],
                           lw["hy_skip"][order:order + 1], cw, cb, B, kh, k1p, g)

    z = inv(conv(proj, col(0), 0, 0), 0, (proj, col(1), 1), (proj, col(0), 0))
    return inv(conv(z, 0, 1, None), 1, (proj, col(2), 2), (z, 0, None), (proj, col(3)))


def _rotary_tables(L):
    inv = ROPE_BASE ** (-jnp.arange(0, DH, 2, dtype=f32) / DH)
    ang = jnp.arange(L, dtype=f32)[:, None] * inv[None, :]
    cos, sin = jnp.cos(ang), jnp.sin(ang)
    return jnp.concatenate([cos, cos], axis=-1), jnp.concatenate([-sin, sin], axis=-1)


def _block_diag(w):
    n, e, fo = w.shape
    eye = jnp.eye(n, dtype=w.dtype)
    return (eye[:, None, :, None] * w[:, :, None, :]).reshape(n * e, n * fo)


def _permute_w_in(w_in):
    ref_gla_small = 2048 + 1536
    ref_hy = ref_gla_small + 2 * GLA_RANK
    ref_ml = ref_hy + 2048
    ref_ml_small = ref_ml + 1536
    main = jnp.concatenate([w_in[..., :ref_gla_small], w_in[..., ref_hy:ref_ml], w_in[..., ref_ml:ref_ml_small]],
                           axis=-1)
    small = jnp.concatenate([w_in[..., ref_gla_small:ref_hy], w_in[..., ref_ml_small:]], axis=-1)
    small = jnp.pad(small, [(0, 0)] * (small.ndim - 1) + [(0, N_SMALL - small.shape[-1])])
    return main.astype(bf16), small.astype(bf16)


def _layer_weights(l, p):
    lw = {}
    lw["gla_wa"] = jnp.stack([
        jnp.pad(p["gla_w_a"][l, d], ((LANE_GLA_A + GLA_RANK * d, 128 - LANE_GLA_A - GLA_RANK * (d + 1)), (0, 0)))
        for d in range(2)]).astype(bf16)
    lw["gla_ba"] = p["gla_b_a"][l]
    lw["gla_ng"] = p["gla_norm_gain"][l].reshape(1, DH)
    lw["hy_conv_w"] = p["hy_conv_w"][l].reshape(3, 3, BR).transpose(1, 0, 2)
    lw["hy_conv_b"] = p["hy_conv_b"][l].reshape(3, 1, BR)
    lw["hy_w1"] = jnp.pad(p["hy_w1"][l], ((0, HY_EMB_PAD - HY_EMB), (0, 0)))
    lw["hy_b1"] = p["hy_b1"][l].reshape(1, HY_HIDDEN)
    lw["hy_w2"] = p["hy_w2"][l]
    lw["hy_b2"] = p["hy_b2"][l].reshape(1, HY_HIDDEN)
    lw["hy_freq"] = p["hy_freq"][l].reshape(1, HY_HIDDEN)
    lw["hy_w3"] = p["hy_w3"][l].reshape(HY_HIDDEN, 2, 2, BR).transpose(2, 0, 1, 3).reshape(2, HY_HIDDEN, 2 * BR)
    deltas = jnp.abs(jnp.linspace(HY_MIN_DECAY, HY_MAX_DECAY, BR, dtype=f32))
    lw["hy_deltas"] = jnp.concatenate([deltas, deltas]).reshape(1, 2 * BR)
    lw["hy_skip"] = p["hy_skip"][l]
    lw["ml_conv_w"] = p["ml_conv_w"][l]
    lw["ml_conv_b"] = p["ml_conv_b"][l].reshape(1, BR)
    lw["ml_wq"] = _block_diag(p["ml_wq"][l]).astype(bf16)
    lw["ml_wk"] = _block_diag(p["ml_wk"][l]).astype(bf16)
    lw["ml_wv"] = _block_diag(p["ml_wv"][l]).astype(bf16)
    bias = jnp.concatenate([p["ml_b_i"][l].reshape(-1), p["ml_b_f"][l].reshape(-1)])
    lw["ml_bias"] = jnp.pad(bias, (LANE_ML_I, 128 - LANE_ML_I - bias.shape[0])).reshape(1, 128)
    lw["ml_ng"] = p["ml_norm_gain"][l].reshape(1, BR)
    lw["ml_skip"] = p["ml_skip"][l].reshape(1, BR)
    return lw


def _mix(proj, small, lw, consts, B, L):
    cosf, sins, emb, tabs = consts
    proj3 = proj.reshape(B, L, N_PROJ)
    small3 = small.reshape(B, L, N_SMALL)
    return (
        _ret_branch(proj3, cosf, sins, B, L).reshape(B * L, BR),
        _gla_branch(proj3, small3, lw["gla_wa"], lw["gla_ba"], lw["gla_ng"], B, L).reshape(B * L, BR),
        _hy_branch(proj, lw, emb, tabs, B, L),
        _ml_branch(proj3, small3, lw["ml_conv_w"], lw["ml_conv_b"], lw["ml_wq"], lw["ml_wk"], lw["ml_wv"],
                   lw["ml_bias"], lw["ml_ng"], lw["ml_skip"], B, L).reshape(B * L, BR),
    )


def kernel(x_prompt, x_sample, c_prompt, c_sample, norm_gain, w_ada, b_ada, w_in, gla_w_a, gla_b_a, gla_norm_gain, hy_conv_w, hy_conv_b, hy_w1, hy_b1, hy_w2, hy_b2, hy_freq, hy_w3, hy_skip, ml_conv_w, ml_conv_b, ml_wq, ml_wk, ml_wv, ml_b_i, ml_b_f, ml_norm_gain, ml_skip, w_branch, w_gate, b_gate, w_out, final_gain):
    p = dict(gla_w_a=gla_w_a, gla_b_a=gla_b_a, gla_norm_gain=gla_norm_gain, hy_conv_w=hy_conv_w,
             hy_conv_b=hy_conv_b, hy_w1=hy_w1, hy_b1=hy_b1, hy_w2=hy_w2, hy_b2=hy_b2, hy_freq=hy_freq,
             hy_w3=hy_w3, hy_skip=hy_skip, ml_conv_w=ml_conv_w, ml_conv_b=ml_conv_b, ml_wq=ml_wq,
             ml_wk=ml_wk, ml_wv=ml_wv, ml_b_i=ml_b_i, ml_b_f=ml_b_f, ml_norm_gain=ml_norm_gain,
             ml_skip=ml_skip)
    groups = []
    for x, c in ((x_prompt, c_prompt), (x_sample, c_sample)):
        B, L, _ = x.shape
        groups.append(dict(B=B, L=L, x=x.reshape(B * L, D_MODEL),
                           consts=_rotary_tables(L) + (_hy_embedding(L), _fft_tables(L))))
    nb = [g["B"] for g in groups]
    rows = -(-sum(nb) // 8) * 8
    c_all = jnp.concatenate([c_prompt, c_sample, jnp.zeros((rows - sum(nb), D_MODEL), f32)], axis=0)
    mod_all = _ada_all(c_all, w_ada, b_ada)

    w_in_p, w_in_s = _permute_w_in(w_in)
    w_in_p = w_in_p.reshape(DEPTH, D_MODEL, N_PROJ // TN_IN, TN_IN).transpose(0, 2, 1, 3)
    ng = D_MODEL // TN_GATES
    wg = w_gate.astype(bf16).reshape(DEPTH, 4, D_MODEL, ng, TN_GATES).transpose(0, 3, 1, 2, 4)
    wb = w_branch.astype(bf16).reshape(DEPTH, 4, BR, ng, TN_GATES).transpose(0, 3, 1, 2, 4)
    bg = b_gate.reshape(DEPTH, 4, 1, ng, TN_GATES).transpose(0, 3, 1, 2, 4)
    wo = w_out.astype(bf16)
    fg = final_gain.reshape(1, D_MODEL)
    for l in range(DEPTH):
        lw = _layer_weights(l, p)
        gain = norm_gain[l].reshape(1, D_MODEL)
        start = 0
        for g in groups:
            B, L = g["B"], g["L"]
            mod = mod_all[l, start:start + B].reshape(B, 3, D_MODEL)
            start += B
            proj, small, h = _inproj(g["x"], mod, gain, w_in_p[l], w_in_s[l], L)
            branches = _mix(proj, small, lw, g["consts"], B, L)
            merged = _gated_merge(h, branches, wg[l], bg[l], wb[l], L)
            g["x"] = _outproj(merged, wo[l], g["x"], mod, fg, L, l == DEPTH - 1)
    return tuple(g["x"].reshape(g["B"], g["L"], D_MODEL) for g in groups)
```

```python
import functools
import math

import jax
import jax.numpy as jnp
from jax import lax
from jax.experimental import pallas as pl
from jax.experimental.pallas import tpu as pltpu

f32 = jnp.float32
bf16 = jnp.bfloat16

D_MODEL = 2048
DEPTH = 4
BR = 512
HEADS = 4
DH = 128
ROPE_BASE = 10000.0
GLA_DK = 64
GLA_KW = 256
GLA_RANK = 16
GLA_GATE_NORM = 16.0
HY_EMB = 33
HY_EMB_PAD = 40
HY_BANDS = 16
HY_HIDDEN = 64
HY_MIN_DECAY = math.log(1e-2) / 1.5
HY_MAX_DECAY = math.log(1e-2) / 0.3
ML_BLOCK = 4
EPS = 1e-6
NEG = -1e30

BLK_RET = 0
BLK_GLA_QK = 4
BLK_GLA_V = 5
BLK_GLA_G = 6
BLK_HY = 7
BLK_ML = 11
N_PROJ = 7168
N_SMALL = 128
LANE_GLA_A = 0
LANE_ML_I = 32
LANE_ML_F = 40

CHUNK = 128
HALO = 16
FFT_N2 = 128
FFT_K1T = 8
FFT_NJ = 4
PITCH_X = FFT_N2 + 8
PITCH_A = 2 * FFT_N2 + 8
TM_IN = 1024
TN_IN = 1792
TM_GATES = 1024
TN_GATES = 512
TM_OUT = 512
SEQ_PER_STEP = 2
VMEM_LIMIT = 60 * 1024 * 1024

LOG_GAMMA = tuple(math.log(1.0 - 2.0 ** (-5.0 - h)) for h in range(HEADS))


def _params(sem):
    return pltpu.CompilerParams(dimension_semantics=sem, vmem_limit_bytes=VMEM_LIMIT)


def _bdot(a, b):
    return jnp.dot(a.astype(bf16), b.astype(bf16), preferred_element_type=f32)


def _bdot_nt(a, b):
    return lax.dot_general(a.astype(bf16), b.astype(bf16), (((1,), (1,)), ((), ())),
                           preferred_element_type=f32)


def _bdot_tn(a, b):
    return jnp.dot(a.T.astype(bf16), b.astype(bf16), preferred_element_type=f32)


def _dot3(a, b):
    a1 = a.astype(bf16)
    ra = a - a1.astype(f32)
    a2 = ra.astype(bf16)
    a3 = (ra - a2.astype(f32)).astype(bf16)
    b1 = b.astype(bf16)
    rb = b - b1.astype(f32)
    b2 = rb.astype(bf16)
    b3 = (rb - b2.astype(f32)).astype(bf16)
    d = functools.partial(jnp.dot, preferred_element_type=f32)
    return (d(a1, b1) + (d(a1, b2) + d(a2, b1))) + ((d(a2, b2) + d(a1, b3)) + d(a3, b1))


def _cumsum01(tri, x):
    x1 = x.astype(bf16)
    r = x - x1.astype(f32)
    x2 = r.astype(bf16)
    x3 = (r - x2.astype(f32)).astype(bf16)
    d = functools.partial(jnp.dot, preferred_element_type=f32)
    return d(tri, x1) + (d(tri, x2) + d(tri, x3))


def _sigmoid(x):
    return jax.nn.sigmoid(x)


def _silu(x):
    return x * jax.nn.sigmoid(x)


def _log_sigmoid(x):
    return jnp.minimum(x, 0.0) - jnp.log1p(jnp.exp(-jnp.abs(x)))


def _tri(n, rev):
    ii = lax.broadcasted_iota(jnp.int32, (n, n), 0)
    jj = lax.broadcasted_iota(jnp.int32, (n, n), 1)
    return (jj >= ii) if rev else (jj <= ii)


def _prev_rows(x, first):
    row = lax.broadcasted_iota(jnp.int32, x.shape, 0)
    return jnp.where(row == 0, first, pltpu.roll(x, 1, 0))


def _next_rows(x, last):
    n = x.shape[0]
    row = lax.broadcasted_iota(jnp.int32, x.shape, 0)
    return jnp.where(row == n - 1, last, pltpu.roll(x, n - 1, 0))


def _conv3(x, prev, nxt, has_prev, has_next, w, b):
    first = jnp.where(has_prev, prev[HALO - 1:HALO, :], 0.0)
    last = jnp.where(has_next, nxt[0:1, :], 0.0)
    return _prev_rows(x, first) * w[0:1] + x * w[1:2] + _next_rows(x, last) * w[2:3] + b


def _head_norm(x, center):
    if center:
        x = x - jnp.mean(x, axis=-1, keepdims=True)
    return x * lax.rsqrt(jnp.mean(x * x, axis=-1, keepdims=True) + EPS)


def _ada_kernel(c_ref, w_ref, b_ref, o_ref):
    o_ref[0] = _bdot(_silu(c_ref[...]), w_ref[0]) + b_ref[0]


def _ada_all(c_all, w_ada, b_ada):
    rows = c_all.shape[0]
    tn = 1536
    return pl.pallas_call(
        _ada_kernel,
        grid=(DEPTH, 3 * D_MODEL // tn),
        in_specs=[
            pl.BlockSpec((rows, D_MODEL), lambda l, j: (0, 0)),
            pl.BlockSpec((1, D_MODEL, tn), lambda l, j: (l, 0, j)),
            pl.BlockSpec((1, 1, tn), lambda l, j: (l, 0, j)),
        ],
        out_specs=pl.BlockSpec((1, rows, tn), lambda l, j: (l, 0, j)),
        out_shape=jax.ShapeDtypeStruct((DEPTH, rows, 3 * D_MODEL), f32),
        compiler_params=_params(("arbitrary", "arbitrary")),
        name="ada",
    )(c_all, w_ada, b_ada.reshape(DEPTH, 1, 3 * D_MODEL))


ROWS_NORM = 256


def _modulated_norm(x, mod, gain):
    y = x * lax.rsqrt(jnp.mean(x * x, axis=-1, keepdims=True) + EPS) * gain
    return y * (1.0 + mod[1:2]) + mod[0:1]


def _norm_to_scratch(x_ref, mod_ref, gain_ref, h_ref):
    mod = mod_ref[0]
    gain = gain_ref[...]

    def step(i, carry):
        rows = pl.ds(pl.multiple_of(i * ROWS_NORM, ROWS_NORM), ROWS_NORM)
        h_ref[rows, :] = _modulated_norm(x_ref[rows, :], mod, gain).astype(bf16)
        return carry

    lax.fori_loop(0, x_ref.shape[0] // ROWS_NORM, step, 0)


def _inproj_kernel(x_ref, mod_ref, gain_ref, w_ref, ws_ref, o_ref, os_ref, h_ref):
    @pl.when(pl.program_id(1) == 0)
    def _():
        _norm_to_scratch(x_ref, mod_ref, gain_ref, h_ref)
        os_ref[...] = jnp.dot(h_ref[...], ws_ref[...], preferred_element_type=f32)

    o_ref[...] = jnp.dot(h_ref[...], w_ref[...], preferred_element_type=f32).astype(bf16)


def _inproj(x, mod, gain, w, ws, L):
    T = x.shape[0]
    tm = min(TM_IN, L)
    return pl.pallas_call(
        _inproj_kernel,
        grid=(T // tm, N_PROJ // TN_IN),
        in_specs=[
            pl.BlockSpec((tm, D_MODEL), lambda i, j: (i, 0)),
            pl.BlockSpec((1, 3, D_MODEL), lambda i, j: ((i * tm) // L, 0, 0)),
            pl.BlockSpec((1, D_MODEL), lambda i, j: (0, 0)),
            pl.BlockSpec((D_MODEL, TN_IN), lambda i, j: (0, j)),
            pl.BlockSpec((D_MODEL, N_SMALL), lambda i, j: (0, 0)),
        ],
        out_specs=[pl.BlockSpec((tm, TN_IN), lambda i, j: (i, j)),
                   pl.BlockSpec((tm, N_SMALL), lambda i, j: (i, 0)),
                   pl.BlockSpec((tm, D_MODEL), lambda i, j: (i, 0))],
        out_shape=[jax.ShapeDtypeStruct((T, N_PROJ), bf16), jax.ShapeDtypeStruct((T, N_SMALL), f32),
                   jax.ShapeDtypeStruct((T, D_MODEL), bf16)],
        compiler_params=_params(("arbitrary", "arbitrary")),
        name="inproj",
    )(x, mod, gain, w, ws)


def _gates_kernel(h_ref, b0_ref, b1_ref, b2_ref, b3_ref, wg_ref, bg_ref, wb_ref, o_ref):
    h = h_ref[...]
    merged = None
    for bi, br_ref in enumerate((b0_ref, b1_ref, b2_ref, b3_ref)):
        g = _sigmoid(jnp.dot(h, wg_ref[0, bi], preferred_element_type=f32) + bg_ref[0, bi])
        t = g * jnp.dot(br_ref[...], wb_ref[0, bi], preferred_element_type=f32)
        merged = t if merged is None else merged + t
    o_ref[...] = merged.astype(bf16)


def _gated_merge(h, branches, wg, bg, wb, L):
    T = h.shape[0]
    tm = min(TM_GATES, L)
    nj, _, _, tn = wg.shape
    br_spec = pl.BlockSpec((tm, BR), lambda i, j: (i, 0))
    return pl.pallas_call(
        _gates_kernel,
        grid=(T // tm, nj),
        in_specs=[
            pl.BlockSpec((tm, D_MODEL), lambda i, j: (i, 0)),
            br_spec, br_spec, br_spec, br_spec,
            pl.BlockSpec((1, 4, D_MODEL, tn), lambda i, j: (j, 0, 0, 0)),
            pl.BlockSpec((1, 4, 1, tn), lambda i, j: (j, 0, 0, 0)),
            pl.BlockSpec((1, 4, BR, tn), lambda i, j: (j, 0, 0, 0)),
        ],
        out_specs=pl.BlockSpec((tm, tn), lambda i, j: (i, j)),
        out_shape=jax.ShapeDtypeStruct((T, D_MODEL), bf16),
        compiler_params=_params(("arbitrary", "arbitrary")),
        name="gates",
    )(h, *branches, wg, bg, wb)


def _outproj_kernel(final, m_ref, wo_ref, x_ref, mod_ref, fg_ref, o_ref):
    y = x_ref[...] + mod_ref[0][2:3] * jnp.dot(m_ref[...], wo_ref[...], preferred_element_type=f32)
    if final:
        y = y * lax.rsqrt(jnp.mean(y * y, axis=-1, keepdims=True) + EPS) * fg_ref[...]
    o_ref[...] = y


def _outproj(merged, wo, x, mod, final_gain, L, final):
    T = x.shape[0]
    tm = min(TM_OUT, L)
    tile = pl.BlockSpec((tm, D_MODEL), lambda i: (i, 0))
    return pl.pallas_call(
        functools.partial(_outproj_kernel, final),
        grid=(T // tm,),
        in_specs=[tile,
                  pl.BlockSpec((D_MODEL, D_MODEL), lambda i: (0, 0), pipeline_mode=pl.Buffered(1)),
                  tile,
                  pl.BlockSpec((1, 3, D_MODEL), lambda i: ((i * tm) // L, 0, 0)),
                  pl.BlockSpec((1, D_MODEL), lambda i: (0, 0))],
        out_specs=tile,
        out_shape=jax.ShapeDtypeStruct((T, D_MODEL), f32),
        compiler_params=_params(("arbitrary",)),
        name="outproj",
    )(merged, wo, x, mod, final_gain)


def _chunk_idx(nc, rev):
    if rev:
        return lambda c: nc - 1 - c
    return lambda c: c


def _tok_spec(ns, nc, rev, blk, width=BR):
    idx = _chunk_idx(nc, rev)
    return pl.BlockSpec((ns, CHUNK, width), lambda b, c: (b, idx(c), blk))


def _per_sequence(body, batched, *refs):
    ns = [r for r, f in zip(refs, batched) if f][0].shape[0]
    for s in range(ns):
        body(*[r.at[s] if f else r for r, f in zip(refs, batched)])


def _seqs_per_step(B):
    return SEQ_PER_STEP if B % SEQ_PER_STEP == 0 else 1


def _pos_spec(nc, rev):
    if rev:
        return pl.BlockSpec((CHUNK, DH), lambda b, c: (nc - 1 - c, 0))
    return pl.BlockSpec((CHUNK, DH), lambda b, c: (c, 0))


def _const_spec(shape):
    nd = len(shape)
    return pl.BlockSpec(shape, lambda b, c: (0,) * nd)


def _ret_kernel(rev, final, batched, *refs):
    _per_sequence(functools.partial(_ret_body, rev, final), batched, *refs)


def _ret_body(rev, final, *refs):
    if final:
        q_ref, k_ref, v_ref, cos_ref, sin_ref, g_ref, rb_ref, o_ref, s_ref = refs
    else:
        q_ref, k_ref, v_ref, cos_ref, sin_ref, o_ref, s_ref = refs
    C = CHUNK

    @pl.when(pl.program_id(1) == 0)
    def _():
        s_ref[...] = jnp.zeros_like(s_ref)

    cos = cos_ref[...]
    sin = sin_ref[...]
    ii = lax.broadcasted_iota(jnp.int32, (C, C), 0)
    jj = lax.broadcasted_iota(jnp.int32, (C, C), 1)
    col = lax.broadcasted_iota(jnp.int32, (C, 1), 0).astype(f32)
    dist = (ii - jj).astype(f32)
    for h in range(HEADS):
        lg = LOG_GAMMA[h]
        sl = slice(h * DH, (h + 1) * DH)
        q = q_ref[:, sl].astype(f32)
        k = k_ref[:, sl].astype(f32)
        v = v_ref[:, sl]
        q = q * cos + pltpu.roll(q, DH // 2, 1) * sin
        k = (k * cos + pltpu.roll(k, DH // 2, 1) * sin) * (DH ** -0.5)
        if rev:
            gam = jnp.where(jj > ii, jnp.exp(-dist * lg), 0.0)
            qd = jnp.exp((C - col) * lg)
            kd = jnp.exp(col * lg)
        else:
            gam = jnp.where(jj <= ii, jnp.exp(dist * lg), 0.0)
            qd = jnp.exp((col + 1.0) * lg)
            kd = jnp.exp((C - 1.0 - col) * lg)
        s_prev = s_ref[h]
        o = _bdot(_bdot_nt(q, k) * gam, v) + qd * _bdot(q, s_prev)
        s_ref[h] = math.exp(C * lg) * s_prev + _bdot_tn(k * kd, v)
        if final:
            r = _head_norm(o + rb_ref[:, sl], center=True)
            o_ref[:, sl] = (r * _silu(g_ref[:, sl].astype(f32))).astype(o_ref.dtype)
        else:
            o_ref[:, sl] = o


def _ret_branch(proj, cosf, sins, B, L):
    nc = L // CHUNK
    ns = _seqs_per_step(B)

    def call(rev, final, extra):
        in_specs = [_tok_spec(ns, nc, rev, BLK_RET + i) for i in range(3)]
        in_specs += [_pos_spec(nc, rev), _pos_spec(nc, rev)]
        args = [proj, proj, proj, cosf, sins]
        batched = [True, True, True, False, False]
        if final:
            in_specs += [_tok_spec(ns, nc, rev, BLK_RET + 3), _tok_spec(ns, nc, rev, 0)]
            args += [proj, extra]
            batched += [True, True]
        batched += [True, True]
        return pl.pallas_call(
            functools.partial(_ret_kernel, rev, final, tuple(batched)),
            grid=(B // ns, nc),
            in_specs=in_specs,
            out_specs=_tok_spec(ns, nc, rev, 0),
            out_shape=jax.ShapeDtypeStruct((B, L, BR), bf16 if final else f32),
            scratch_shapes=[pltpu.VMEM((ns, HEADS, DH, DH), f32)],
            compiler_params=_params(("arbitrary", "arbitrary")),
            name="ret_fwd" if final else "ret_bwd",
        )(*args)

    rb = call(True, False, None)
    return call(False, True, rb)


def _gla_kernel(rev, final, batched, *refs):
    _per_sequence(functools.partial(_gla_body, rev, final), batched, *refs)


def _gla_body(rev, final, *refs):
    if final:
        qk_ref, v_ref, small_ref, wa_ref, ba_ref, g_ref, ob_ref, ng_ref, o_ref, st_ref = refs
    else:
        qk_ref, v_ref, small_ref, wa_ref, ba_ref, o_ref, st_ref = refs
    C = CHUNK

    @pl.when(pl.program_id(1) == 0)
    def _():
        st_ref[...] = jnp.zeros_like(st_ref)

    q = qk_ref[:, :GLA_KW].astype(f32) * (GLA_DK ** -0.5)
    k = qk_ref[:, GLA_KW:].astype(f32)
    v = v_ref[...].astype(f32)
    tri = _tri(C, rev)
    log_a = _log_sigmoid(_bdot(small_ref[...], wa_ref[...]) + ba_ref[...]) / GLA_GATE_NORM
    b = _cumsum01(tri.astype(bf16), log_a)
    ref_row = b[C // 2:C // 2 + 1, :]
    b_end = b[0:1, :] if rev else b[C - 1:C, :]
    qt = q * jnp.exp(b - ref_row)
    kt = k * jnp.exp(ref_row - b)
    inter = _bdot_nt(q * jnp.exp(b), st_ref[...])
    lane = lax.broadcasted_iota(jnp.int32, (1, GLA_KW), 1)
    for h in range(HEADS):
        sl = slice(h * DH, (h + 1) * DH)
        a = _bdot_nt(jnp.where(lane // GLA_DK == h, qt, 0.0), kt)
        o = _bdot(jnp.where(tri, a, 0.0), v[:, sl]) + inter[:, sl]
        if final:
            o = _head_norm(o + ob_ref[:, sl], center=False) * ng_ref[...]
            o_ref[:, sl] = (o * _silu(g_ref[:, sl].astype(f32))).astype(o_ref.dtype)
        else:
            o_ref[:, sl] = o
    ee = lax.broadcasted_iota(jnp.int32, (BR, GLA_KW), 0) // DH
    dd = lax.broadcasted_iota(jnp.int32, (BR, GLA_KW), 1) // GLA_DK
    upd = _bdot_tn(v, k * jnp.exp(b_end - b))
    st_ref[...] = st_ref[...] * jnp.exp(b_end) + jnp.where(ee == dd, upd, 0.0)


def _gla_branch(proj, small, wa_pad, ba, norm_gain, B, L):
    nc = L // CHUNK
    ns = _seqs_per_step(B)

    def call(rev, final, extra):
        d = 1 if rev else 0
        in_specs = [_tok_spec(ns, nc, rev, BLK_GLA_QK), _tok_spec(ns, nc, rev, BLK_GLA_V),
                    _tok_spec(ns, nc, rev, 0, N_SMALL),
                    _const_spec((128, GLA_KW)), _const_spec((1, GLA_KW))]
        args = [proj, proj, small, wa_pad[d], ba[d:d + 1]]
        batched = [True, True, True, False, False]
        if final:
            in_specs += [_tok_spec(ns, nc, rev, BLK_GLA_G), _tok_spec(ns, nc, rev, 0), _const_spec((1, DH))]
            args += [proj, extra, norm_gain]
            batched += [True, True, False]
        batched += [True, True]
        return pl.pallas_call(
            functools.partial(_gla_kernel, rev, final, tuple(batched)),
            grid=(B // ns, nc),
            in_specs=in_specs,
            out_specs=_tok_spec(ns, nc, rev, 0),
            out_shape=jax.ShapeDtypeStruct((B, L, BR), bf16 if final else f32),
            scratch_shapes=[pltpu.VMEM((ns, BR, GLA_KW), f32)],
            compiler_params=_params(("arbitrary", "arbitrary")),
            name="gla_fwd" if final else "gla_bwd",
        )(*args)

    ob = call(True, False, None)
    return call(False, True, ob)


def _ml_kernel(rev, final, batched, *refs):
    _per_sequence(functools.partial(_ml_body, rev, final), batched, *refs)


def _ml_body(rev, final, *refs):
    if final:
        (u_ref, up_ref, un_ref, small_ref, cw_ref, cb_ref, wq_ref, wk_ref, wv_ref, bias_ref,
         zg_ref, op_ref, hb_ref, ng_ref, skip_ref, o_ref, s_ref, n_ref, m_ref) = refs
    else:
        (u_ref, up_ref, un_ref, small_ref, cw_ref, cb_ref, wq_ref, wk_ref, wv_ref, bias_ref,
         o_ref, s_ref, n_ref, m_ref) = refs
    C = CHUNK
    c = pl.program_id(1)
    nc = pl.num_programs(1)

    @pl.when(c == 0)
    def _():
        s_ref[...] = jnp.zeros_like(s_ref)
        n_ref[...] = jnp.zeros_like(n_ref)
        m_ref[...] = jnp.full(m_ref.shape, NEG, f32)

    pos = (nc - 1 - c) if rev else c
    u = u_ref[...].astype(f32)
    xc = _silu(_conv3(u, up_ref[...].astype(f32), un_ref[...].astype(f32), pos > 0, pos < nc - 1,
                      cw_ref[...], cb_ref[...]))
    q = _bdot(xc, wq_ref[...])
    k = _bdot(xc, wk_ref[...]) * (DH ** -0.5)
    v = _bdot(u, wv_ref[...])

    gates = small_ref[...] + bias_ref[...]
    log_f = _log_sigmoid(gates)
    tri = _tri(C, rev)
    cum = _cumsum01(tri.astype(bf16), log_f)
    gates_t = gates.T
    cum_t = cum.T
    d = 1 if rev else 0
    for h in range(HEADS):
        sl = slice(h * DH, (h + 1) * DH)
        li = LANE_ML_I + 4 * d + h
        lf = LANE_ML_F + 4 * d + h
        b_col = cum[:, lf:lf + 1]
        b_row = cum_t[lf:lf + 1, :]
        i_col = gates[:, li:li + 1]
        i_row = gates_t[li:li + 1, :]
        b_end = b_col[0:1, :] if rev else b_col[C - 1:C, :]
        m_prev = m_ref[h][0:1, 0:1]
        s_prev = s_ref[h]
        n_prev = n_ref[h][0:1, :]
        qh = q[:, sl]
        kh = k[:, sl]
        vh = v[:, sl]
        dmat = jnp.where(tri, b_col - b_row + i_row, NEG)
        m_in = b_col + m_prev
        m_t = jnp.maximum(m_in, jnp.max(dmat, axis=1, keepdims=True))
        p = jnp.exp(dmat - m_t) * _bdot_nt(qh, kh)
        a_in = jnp.exp(m_in - m_t)
        num = _bdot(p, vh) + a_in * _bdot(qh, s_prev)
        den = jnp.sum(p, axis=1, keepdims=True) + a_in * jnp.sum(qh * n_prev, axis=1, keepdims=True)
        hd = num / jnp.maximum(jnp.abs(den), jnp.exp(-m_t))
        g_col = b_end - b_col + i_col
        m_loc = jnp.max(g_col, axis=0, keepdims=True)
        kw = kh * jnp.exp(g_col - m_loc)
        m_new = jnp.maximum(b_end + m_prev, m_loc)
        a_old = jnp.exp(b_end + m_prev - m_new)
        a_new = jnp.exp(m_loc - m_new)
        s_ref[h] = a_old * s_prev + a_new * _bdot_tn(kw, vh)
        n_ref[h] = jnp.broadcast_to(a_old * n_prev + a_new * jnp.sum(kw, axis=0, keepdims=True), (8, DH))
        m_ref[h] = jnp.broadcast_to(m_new, (8, DH))
        if final:
            hh = _sigmoid(op_ref[:, sl].astype(f32)) * (hd + hb_ref[:, sl])
            hh = _head_norm(hh, center=True) * ng_ref[:, sl] + skip_ref[:, sl] * xc[:, sl]
            o_ref[:, sl] = (hh * _silu(zg_ref[:, sl].astype(f32))).astype(o_ref.dtype)
        else:
            o_ref[:, sl] = hd


def _ml_branch(proj, small, conv_w, conv_b, wq, wk, wv, bias_small, norm_gain, skip, B, L):
    nc = L // CHUNK
    ns = _seqs_per_step(B)
    per = CHUNK // HALO
    last = L // HALO - 1

    def call(rev, final, extra):
        idx = _chunk_idx(nc, rev)
        prev_spec = pl.BlockSpec((ns, HALO, BR), lambda b, c: (b, jnp.maximum(idx(c) * per - 1, 0), BLK_ML))
        next_spec = pl.BlockSpec((ns, HALO, BR), lambda b, c: (b, jnp.minimum((idx(c) + 1) * per, last), BLK_ML))
        in_specs = [_tok_spec(ns, nc, rev, BLK_ML), prev_spec, next_spec, _tok_spec(ns, nc, rev, 0, N_SMALL),
                    _const_spec((3, BR)), _const_spec((1, BR)),
                    _const_spec((BR, BR)), _const_spec((BR, BR)), _const_spec((BR, BR)),
                    _const_spec((1, 128))]
        args = [proj, proj, proj, small, conv_w, conv_b, wq, wk, wv, bias_small]
        batched = [True] * 4 + [False] * 6
        if final:
            in_specs += [_tok_spec(ns, nc, rev, BLK_ML + 1), _tok_spec(ns, nc, rev, BLK_ML + 2),
                         _tok_spec(ns, nc, rev, 0), _const_spec((1, BR)), _const_spec((1, BR))]
            args += [proj, proj, extra, norm_gain, skip]
            batched += [True, True, True, False, False]
        batched += [True] * 4
        return pl.pallas_call(
            functools.partial(_ml_kernel, rev, final, tuple(batched)),
            grid=(B // ns, nc),
            in_specs=in_specs,
            out_specs=_tok_spec(ns, nc, rev, 0),
            out_shape=jax.ShapeDtypeStruct((B, L, BR), bf16 if final else f32),
            scratch_shapes=[pltpu.VMEM((ns, HEADS, DH, DH), f32), pltpu.VMEM((ns, HEADS, 8, DH), f32),
                            pltpu.VMEM((ns, HEADS, 8, DH), f32)],
            compiler_params=_params(("arbitrary", "arbitrary")),
            name="ml_fwd" if final else "ml_bwd",
        )(*args)

    hb = call(True, False, None)
    return call(False, True, hb)


def _hy_filt_kernel(emb_ref, w1_ref, b1_ref, w2_ref, b2_ref, fr_ref, w3_ref, dl_ref, taps_ref, sc_ref):
    r = pl.program_id(0)

    @pl.when(r == 0)
    def _():
        sc_ref[...] = jnp.zeros_like(sc_ref)

    emb = emb_ref[...]
    fr = fr_ref[...]
    hmid = jnp.sin(fr * (_dot3(emb, w1_ref[...]) + b1_ref[...]))
    hmid = jnp.sin(fr * (_dot3(hmid, w2_ref[...]) + b2_ref[...]))
    hout = _dot3(hmid, w3_ref[0])
    t = emb[:, 0:1]
    valid = emb[:, HY_EMB:HY_EMB + 1]
    taps = hout * jnp.exp(-t * dl_ref[...]) * valid
    taps_ref[...] = taps.astype(taps_ref.dtype)
    sc_ref[...] += jnp.sum(taps * taps, axis=0, keepdims=True)

    @pl.when(r == pl.num_programs(0) - 1)
    def _():
        sc_ref[...] = lax.rsqrt(sc_ref[...] + EPS)


def _hy_filter(emb, w1p, b1, w2, b2, freq, w3d, deltas2, L):
    n = 2 * L
    tr = 512
    half = L // tr
    return pl.pallas_call(
        _hy_filt_kernel,
        grid=(n // tr,),
        in_specs=[
            pl.BlockSpec((tr, HY_EMB_PAD), lambda r: (r, 0)),
            pl.BlockSpec((HY_EMB_PAD, HY_HIDDEN), lambda r: (0, 0)),
            pl.BlockSpec((1, HY_HIDDEN), lambda r: (0, 0)),
            pl.BlockSpec((HY_HIDDEN, HY_HIDDEN), lambda r: (0, 0)),
            pl.BlockSpec((1, HY_HIDDEN), lambda r: (0, 0)),
            pl.BlockSpec((1, HY_HIDDEN), lambda r: (0, 0)),
            pl.BlockSpec((1, HY_HIDDEN, 2 * BR), lambda r: (r // half, 0, 0)),
            pl.BlockSpec((1, 2 * BR), lambda r: (0, 0)),
        ],
        out_specs=[pl.BlockSpec((tr, 2 * BR), lambda r: (r, 0)),
                   pl.BlockSpec((1, 2 * BR), lambda r: (0, 0))],
        out_shape=[jax.ShapeDtypeStruct((n, 2 * BR), bf16), jax.ShapeDtypeStruct((1, 2 * BR), f32)],
        compiler_params=_params(("arbitrary",)),
        name="hy_filter",
    )(emb, w1p, b1, w2, b2, freq, w3d, deltas2)


def _seq_chunk(ref, i, nchunks, w, b):
    r0 = pl.multiple_of(i * FFT_N2, FFT_N2)
    x = ref[pl.ds(r0, FFT_N2), :].astype(f32)
    if w is None:
        return x
    total = nchunks * FFT_N2
    p0 = pl.multiple_of(jnp.maximum(r0 - HALO, 0), HALO)
    n0 = pl.multiple_of(jnp.minimum(r0 + FFT_N2, total - HALO), HALO)
    prev = ref[pl.ds(p0, HALO), :].astype(f32)
    nxt = ref[pl.ds(n0, HALO), :].astype(f32)
    return _conv3(x, prev, nxt, i > 0, i < nchunks - 1, w, b)


def _dft_outer_kernel(K, k1t, conv, *refs):
    if conv:
        x_ref, cw_ref, cb_ref, f_ref, o_ref, xs_ref = refs
        w, b = cw_ref[0], cb_ref[0]
    else:
        x_ref, f_ref, o_ref, xs_ref = refs
        w = b = None

    @pl.when(pl.program_id(2) == 0)
    def _():
        def fill(i, carry):
            xs_ref[pl.ds(pl.multiple_of(i * PITCH_X, 8), FFT_N2), :] = _seq_chunk(x_ref, i, K, w, b)
            return carry
        lax.fori_loop(0, K, fill, 0)

    f = f_ref[0]

    def body(g, carry):
        j0 = g * FFT_NJ
        x = jnp.concatenate([xs_ref[pl.ds(j0 + i, K, stride=PITCH_X), :] for i in range(FFT_NJ)], axis=1)
        r = jnp.dot(f, x.astype(bf16), preferred_element_type=f32)
        for i in range(FFT_NJ):
            sl = slice(i * 128, (i + 1) * 128)
            o_ref[pl.ds(j0 + i, k1t, stride=PITCH_A), :] = r[:k1t, sl]
            o_ref[pl.ds(FFT_N2 + j0 + i, k1t, stride=PITCH_A), :] = r[k1t:, sl]
        return carry

    lax.fori_loop(0, FFT_N2 // FFT_NJ, body, 0, unroll=2)
    zero = jnp.zeros((k1t, 128), f32)
    for i in range(PITCH_A - 2 * FFT_N2):
        o_ref[pl.ds(2 * FFT_N2 + i, k1t, stride=PITCH_A), :] = zero


def _dft_outer(x, col0, ncol, fmat, B, K, K1p, k1t, conv=None):
    nt = K1p // k1t
    in_specs = [pl.BlockSpec((K * FFT_N2, 128), lambda b, c, t: (b, col0 + c))]
    args = [x]
    if conv is not None:
        cw, cb, s = conv
        in_specs += [pl.BlockSpec((1, 3, 128), lambda b, c, t: (s, 0, c)),
                     pl.BlockSpec((1, 1, 128), lambda b, c, t: (s, 0, c))]
        args += [cw, cb]
    in_specs.append(pl.BlockSpec((1, 2 * k1t, K), lambda b, c, t: (t, 0, 0)))
    args.append(fmat)
    return pl.pallas_call(
        functools.partial(_dft_outer_kernel, K, k1t, conv is not None),
        grid=(B, ncol, nt),
        in_specs=in_specs,
        out_specs=pl.BlockSpec((k1t * PITCH_A, 128), lambda b, c, t: (b * nt + t, c)),
        out_shape=jax.ShapeDtypeStruct((B * K1p * PITCH_A, 128 * ncol), f32),
        scratch_shapes=[pltpu.VMEM((K * PITCH_X, 128), f32)],
        compiler_params=_params(("arbitrary", "arbitrary", "arbitrary")),
        name="dft_outer",
    )(*args)


def _dft_inner_kernel(conv, *refs):
    if conv:
        a_ref, gf_ref, h_ref, gi_ref, o_ref = refs
    else:
        a_ref, gf_ref, o_ref = refs
    n2 = FFT_N2
    dot = functools.partial(jnp.dot, preferred_element_type=f32)
    for kk in range(FFT_K1T):
        gf = gf_ref[kk]
        x = dot(gf[:, :n2], a_ref[0, kk, :n2].astype(bf16)) + dot(gf[:, n2:], a_ref[0, kk, n2:2 * n2].astype(bf16))
        if conv:
            xr, xi = x[:n2], x[n2:]
            hr, hi = h_ref[0, kk, :n2], h_ref[0, kk, n2:2 * n2]
            yr = (xr * hr - xi * hi).astype(bf16)
            yi = (xr * hi + xi * hr).astype(bf16)
            gi = gi_ref[kk]
            x = dot(gi[:, :n2], yr) + dot(gi[:, n2:], yi)
        o_ref[0, kk, :2 * n2] = x
        o_ref[0, kk, 2 * n2:] = jnp.zeros((PITCH_A - 2 * n2, BR), f32)


def _dft_inner(a4, gf, h4=None, gi=None, hblk=0):
    B, k1p, pa, W = a4.shape
    conv = h4 is not None
    kt = FFT_K1T
    r2 = 2 * FFT_N2
    in_specs = [pl.BlockSpec((1, kt, pa, BR), lambda k, w, b: (b, k, 0, w)),
                pl.BlockSpec((kt, r2, r2), lambda k, w, b: (k, 0, 0))]
    args = [a4, gf]
    if conv:
        in_specs += [pl.BlockSpec((1, kt, pa, BR), lambda k, w, b: (0, k, 0, hblk)),
                     pl.BlockSpec((kt, r2, r2), lambda k, w, b: (k, 0, 0))]
        args += [h4, gi]
    return pl.pallas_call(
        functools.partial(_dft_inner_kernel, conv),
        grid=(k1p // kt, W // BR, B),
        in_specs=in_specs,
        out_specs=pl.BlockSpec((1, kt, pa, BR), lambda k, w, b: (b, k, 0, w)),
        out_shape=jax.ShapeDtypeStruct((B, k1p, pa, W), f32),
        compiler_params=_params(("arbitrary", "arbitrary", "arbitrary")),
        name="dft_inner_conv" if conv else "dft_inner_spec",
    )(*args)


def _idft_outer_kernel(kh, K1p, final, conv_s, *refs):
    it = iter(refs)
    b_ref, fr_ref, fi_ref, mul_ref, s_ref, sc_ref, sk_ref, cwm_ref, cbm_ref = [next(it) for _ in range(9)]
    cws_ref, cbs_ref = (next(it), next(it)) if conv_s else (None, None)
    g_ref = next(it) if final else None
    o_ref, ys_ref = next(it), next(it)
    fr = fr_ref[...]
    fi = fi_ref[...]
    dot = functools.partial(jnp.dot, preferred_element_type=f32)

    def body(g, carry):
        j0 = g * FFT_NJ
        br = jnp.concatenate([b_ref[pl.ds(j0 + i, K1p, stride=PITCH_A), :] for i in range(FFT_NJ)], axis=1)
        bi = jnp.concatenate([b_ref[pl.ds(FFT_N2 + j0 + i, K1p, stride=PITCH_A), :] for i in range(FFT_NJ)], axis=1)
        pad = fr.shape[1] - K1p
        if pad:
            zero = jnp.zeros((pad, br.shape[1]), f32)
            br = jnp.concatenate([br, zero], axis=0)
            bi = jnp.concatenate([bi, zero], axis=0)
        y = dot(fr, br.astype(bf16)) + dot(fi, bi.astype(bf16))
        for i in range(FFT_NJ):
            ys_ref[pl.ds(j0 + i, kh, stride=PITCH_X), :] = y[:, i * 128:(i + 1) * 128]
        return carry

    lax.fori_loop(0, FFT_N2 // FFT_NJ, body, 0, unroll=2)
    sc = sc_ref[...]
    sk = sk_ref[...]

    def epi(i, carry):
        y = ys_ref[pl.ds(pl.multiple_of(i * PITCH_X, 8), FFT_N2), :]
        m = _seq_chunk(mul_ref, i, kh, cwm_ref[0], cbm_ref[0])
        s = _seq_chunk(s_ref, i, kh, cws_ref[0], cbs_ref[0]) if conv_s else _seq_chunk(s_ref, i, kh, None, None)
        z = m * (y * sc + s * sk)
        if final:
            z = z * _silu(_seq_chunk(g_ref, i, kh, None, None))
        o_ref[pl.ds(pl.multiple_of(i * FFT_N2, FFT_N2), FFT_N2), :] = z.astype(o_ref.dtype)
        return carry

    lax.fori_loop(0, kh, epi, 0)


def _idft_outer(b2, fr, fi, mul, s, scale, skip, conv_w, conv_b, B, kh, K1p, g=None):
    L = kh * FFT_N2
    once = pl.Buffered(1) if L > 4096 else None
    mul_a, mul_c, mul_s = mul
    s_a, s_c, s_s = s
    conv_s = s_s is not None
    vec = pl.BlockSpec((1, 128), lambda b, c: (0, c))
    in_specs = [pl.BlockSpec((K1p * PITCH_A, 128), lambda b, c: (b, c), pipeline_mode=once),
                pl.BlockSpec(fr.shape, lambda b, c: (0, 0)), pl.BlockSpec(fi.shape, lambda b, c: (0, 0)),
                pl.BlockSpec((L, 128), lambda b, c: (b, mul_c + c), pipeline_mode=once),
                pl.BlockSpec((L, 128), lambda b, c: (b, s_c + c), pipeline_mode=once),
                vec, vec,
                pl.BlockSpec((1, 3, 128), lambda b, c: (mul_s, 0, c)),
                pl.BlockSpec((1, 1, 128), lambda b, c: (mul_s, 0, c))]
    args = [b2, fr, fi, mul_a, s_a, scale, skip, conv_w, conv_b]
    if conv_s:
        in_specs += [pl.BlockSpec((1, 3, 128), lambda b, c: (s_s, 0, c)),
                     pl.BlockSpec((1, 1, 128), lambda b, c: (s_s, 0, c))]
        args += [conv_w, conv_b]
    if g is not None:
        g_a, g_c = g
        in_specs.append(pl.BlockSpec((L, 128), lambda b, c: (b, g_c + c), pipeline_mode=once))
        args.append(g_a)
    return pl.pallas_call(
        functools.partial(_idft_outer_kernel, kh, K1p, g is not None, conv_s),
        grid=(B, BR // 128),
        in_specs=in_specs,
        out_specs=pl.BlockSpec((L, 128), lambda b, c: (b, c)),
        out_shape=jax.ShapeDtypeStruct((B * L, BR), bf16),
        scratch_shapes=[pltpu.VMEM((kh * PITCH_X, 128), f32)],
        compiler_params=_params(("arbitrary", "arbitrary")),
        name="idft_outer",
    )(*args)


def _fft_tables(L):
    n = 2 * L
    n2 = FFT_N2
    n1 = n // n2
    kh = n1 // 2
    k1t = 48 if kh + 1 > 48 else -(-(kh + 1) // 8) * 8
    k1p = -(-(kh + 1) // k1t) * k1t
    nt = k1p // k1t
    two_pi = 2.0 * math.pi
    k1 = jnp.arange(k1p, dtype=jnp.int32)

    def outer(K):
        m = jnp.arange(K, dtype=jnp.int32)
        ang = (two_pi / n1) * ((k1[:, None] * m[None, :]) % n1).astype(f32)
        re = jnp.cos(ang).reshape(nt, k1t, K)
        im = (-jnp.sin(ang)).reshape(nt, k1t, K)
        return jnp.concatenate([re, im], axis=1).astype(bf16)

    r1 = jnp.arange(kh, dtype=jnp.int32)
    wgt = jnp.where((k1 == 0) | (k1 == kh), 1.0, jnp.where(k1 < kh, 2.0, 0.0)).astype(f32)
    angi = (two_pi / n1) * ((r1[:, None] * k1[None, :]) % n1).astype(f32)
    kpad = ((0, 0), (0, -k1p % 16))
    fi_re = jnp.pad(jnp.cos(angi) * wgt[None, :], kpad).astype(bf16)
    fi_im = jnp.pad(-jnp.sin(angi) * wgt[None, :], kpad).astype(bf16)
    kk = k1[:, None, None]
    a = jnp.arange(n2, dtype=jnp.int32)[None, :, None]
    c = jnp.arange(n2, dtype=jnp.int32)[None, None, :]
    angp = (two_pi / n) * ((a * c * n1 + c * kk) % n).astype(f32)
    mr, mi = jnp.cos(angp), -jnp.sin(angp)
    g_fwd = jnp.concatenate([jnp.concatenate([mr, -mi], axis=2),
                             jnp.concatenate([mi, mr], axis=2)], axis=1).astype(bf16)
    angq = (two_pi / n) * ((a * c * n1 + a * kk) % n).astype(f32)
    vr, vi = jnp.cos(angq), jnp.sin(angq)
    g_inv = jnp.concatenate([jnp.concatenate([vr, -vi], axis=2),
                             jnp.concatenate([vi, vr], axis=2)], axis=1).astype(bf16)
    return dict(n=n, n1=n1, kh=kh, k1p=k1p, k1t=k1t, f_half=outer(kh), f_full=outer(n1),
                fi_re=fi_re, fi_im=fi_im, g_fwd=g_fwd, g_inv=g_inv)


def _hy_embedding(L):
    n = jnp.arange(2 * L, dtype=jnp.int32)
    pos = jnp.where(n < L, n, 2 * L - n).astype(f32)
    t = pos / (L - 1)
    f = jnp.linspace(1e-4, HY_BANDS - 1, HY_BANDS, dtype=f32)
    ang = (2.0 * math.pi / L) * pos[:, None] * f[None, :]
    valid = (n != L).astype(f32)
    pad = jnp.zeros((2 * L, HY_EMB_PAD - HY_EMB - 1), f32)
    return jnp.concatenate([t[:, None], jnp.cos(ang), -jnp.sin(ang), valid[:, None], pad], axis=-1)


def _hy_branch(proj, lw, emb, tabs, B, L):
    n1, kh, k1p, k1t = tabs["n1"], tabs["kh"], tabs["k1p"], tabs["k1t"]
    cw, cb = lw["hy_conv_w"], lw["hy_conv_b"]
    taps, scale = _hy_filter(emb, lw["hy_w1"], lw["hy_b1"], lw["hy_w2"], lw["hy_b2"], lw["hy_freq"],
                             lw["hy_w3"], lw["hy_deltas"], L)
    ta = _dft_outer(taps, 0, 2 * BR // 128, tabs["f_full"], 1, n1, k1p, k1t)
    h4 = _dft_inner(ta.reshape(1, k1p, PITCH_A, 2 * BR), tabs["g_fwd"])
    scale = scale * (1.0 / tabs["n"])
    col = lambda s: (BLK_HY + s) * (BR // 128)

    def conv(x, col0, order, stream):
        a = _dft_outer(x, col0, BR // 128, tabs["f_half"], B, kh, k1p, k1t,
                       None if stream is None else (cw, cb, stream))
        b4 = _dft_inner(a.reshape(B, k1p, PITCH_A, BR), tabs["g_fwd"], h4, tabs["g_inv"], order)
        return b4.reshape(B * k1p * PITCH_A, BR)

    def inv(b2, order, mul, s, g=None):
        return _idft_outer(b2, tabs["fi_re"], tabs["fi_im"], mul, s, scale[:, order * BR:(order + 1) * BR],
                           lw["hy_skip"][order:order + 1], cw, cb, B, kh, k1p, g)

    z = inv(conv(proj, col(0), 0, 0), 0, (proj, col(1), 1), (proj, col(0), 0))
    return inv(conv(z, 0, 1, None), 1, (proj, col(2), 2), (z, 0, None), (proj, col(3)))


def _rotary_tables(L):
    inv = ROPE_BASE ** (-jnp.arange(0, DH, 2, dtype=f32) / DH)
    ang = jnp.arange(L, dtype=f32)[:, None] * inv[None, :]
    cos, sin = jnp.cos(ang), jnp.sin(ang)
    return jnp.concatenate([cos, cos], axis=-1), jnp.concatenate([-sin, sin], axis=-1)


def _block_diag(w):
    n, e, fo = w.shape
    eye = jnp.eye(n, dtype=w.dtype)
    return (eye[:, None, :, None] * w[:, :, None, :]).reshape(n * e, n * fo)


def _permute_w_in(w_in):
    ref_gla_small = 2048 + 1536
    ref_hy = ref_gla_small + 2 * GLA_RANK
    ref_ml = ref_hy + 2048
    ref_ml_small = ref_ml + 1536
    main = jnp.concatenate([w_in[..., :ref_gla_small], w_in[..., ref_hy:ref_ml], w_in[..., ref_ml:ref_ml_small]],
                           axis=-1)
    small = jnp.concatenate([w_in[..., ref_gla_small:ref_hy], w_in[..., ref_ml_small:]], axis=-1)
    small = jnp.pad(small, [(0, 0)] * (small.ndim - 1) + [(0, N_SMALL - small.shape[-1])])
    return main.astype(bf16), small.astype(bf16)


def _layer_weights(l, p):
    lw = {}
    lw["gla_wa"] = jnp.stack([
        jnp.pad(p["gla_w_a"][l, d], ((LANE_GLA_A + GLA_RANK * d, 128 - LANE_GLA_A - GLA_RANK * (d + 1)), (0, 0)))
        for d in range(2)]).astype(bf16)
    lw["gla_ba"] = p["gla_b_a"][l]
    lw["gla_ng"] = p["gla_norm_gain"][l].reshape(1, DH)
    lw["hy_conv_w"] = p["hy_conv_w"][l].reshape(3, 3, BR).transpose(1, 0, 2)
    lw["hy_conv_b"] = p["hy_conv_b"][l].reshape(3, 1, BR)
    lw["hy_w1"] = jnp.pad(p["hy_w1"][l], ((0, HY_EMB_PAD - HY_EMB), (0, 0)))
    lw["hy_b1"] = p["hy_b1"][l].reshape(1, HY_HIDDEN)
    lw["hy_w2"] = p["hy_w2"][l]
    lw["hy_b2"] = p["hy_b2"][l].reshape(1, HY_HIDDEN)
    lw["hy_freq"] = p["hy_freq"][l].reshape(1, HY_HIDDEN)
    lw["hy_w3"] = p["hy_w3"][l].reshape(HY_HIDDEN, 2, 2, BR).transpose(2, 0, 1, 3).reshape(2, HY_HIDDEN, 2 * BR)
    deltas = jnp.abs(jnp.linspace(HY_MIN_DECAY, HY_MAX_DECAY, BR, dtype=f32))
    lw["hy_deltas"] = jnp.concatenate([deltas, deltas]).reshape(1, 2 * BR)
    lw["hy_skip"] = p["hy_skip"][l]
    lw["ml_conv_w"] = p["ml_conv_w"][l]
    lw["ml_conv_b"] = p["ml_conv_b"][l].reshape(1, BR)
    lw["ml_wq"] = _block_diag(p["ml_wq"][l]).astype(bf16)
    lw["ml_wk"] = _block_diag(p["ml_wk"][l]).astype(bf16)
    lw["ml_wv"] = _block_diag(p["ml_wv"][l]).astype(bf16)
    bias = jnp.concatenate([p["ml_b_i"][l].reshape(-1), p["ml_b_f"][l].reshape(-1)])
    lw["ml_bias"] = jnp.pad(bias, (LANE_ML_I, 128 - LANE_ML_I - bias.shape[0])).reshape(1, 128)
    lw["ml_ng"] = p["ml_norm_gain"][l].reshape(1, BR)
    lw["ml_skip"] = p["ml_skip"][l].reshape(1, BR)
    return lw


def _mix(proj, small, lw, consts, B, L):
    cosf, sins, emb, tabs = consts
    proj3 = proj.reshape(B, L, N_PROJ)
    small3 = small.reshape(B, L, N_SMALL)
    return (
        _ret_branch(proj3, cosf, sins, B, L).reshape(B * L, BR),
        _gla_branch(proj3, small3, lw["gla_wa"], lw["gla_ba"], lw["gla_ng"], B, L).reshape(B * L, BR),
        _hy_branch(proj, lw, emb, tabs, B, L),
        _ml_branch(proj3, small3, lw["ml_conv_w"], lw["ml_conv_b"], lw["ml_wq"], lw["ml_wk"], lw["ml_wv"],
                   lw["ml_bias"], lw["ml_ng"], lw["ml_skip"], B, L).reshape(B * L, BR),
    )


def kernel(x_prompt, x_sample, c_prompt, c_sample, norm_gain, w_ada, b_ada, w_in, gla_w_a, gla_b_a, gla_norm_gain, hy_conv_w, hy_conv_b, hy_w1, hy_b1, hy_w2, hy_b2, hy_freq, hy_w3, hy_skip, ml_conv_w, ml_conv_b, ml_wq, ml_wk, ml_wv, ml_b_i, ml_b_f, ml_norm_gain, ml_skip, w_branch, w_gate, b_gate, w_out, final_gain):
    p = dict(gla_w_a=gla_w_a, gla_b_a=gla_b_a, gla_norm_gain=gla_norm_gain, hy_conv_w=hy_conv_w,
             hy_conv_b=hy_conv_b, hy_w1=hy_w1, hy_b1=hy_b1, hy_w2=hy_w2, hy_b2=hy_b2, hy_freq=hy_freq,
             hy_w3=hy_w3, hy_skip=hy_skip, ml_conv_w=ml_conv_w, ml_conv_b=ml_conv_b, ml_wq=ml_wq,
             ml_wk=ml_wk, ml_wv=ml_wv, ml_b_i=ml_b_i, ml_b_f=ml_b_f, ml_norm_gain=ml_norm_gain,
             ml_skip=ml_skip)
    groups = []
    for x, c in ((x_prompt, c_prompt), (x_sample, c_sample)):
        B, L, _ = x.shape
        groups.append(dict(B=B, L=L, x=x.reshape(B * L, D_MODEL),
                           consts=_rotary_tables(L) + (_hy_embedding(L), _fft_tables(L))))
    nb = [g["B"] for g in groups]
    rows = -(-sum(nb) // 8) * 8
    c_all = jnp.concatenate([c_prompt, c_sample, jnp.zeros((rows - sum(nb), D_MODEL), f32)], axis=0)
    mod_all = _ada_all(c_all, w_ada, b_ada)

    w_in_p, w_in_s = _permute_w_in(w_in)
    ng = D_MODEL // TN_GATES
    wg = w_gate.astype(bf16).reshape(DEPTH, 4, D_MODEL, ng, TN_GATES).transpose(0, 3, 1, 2, 4)
    wb = w_branch.astype(bf16).reshape(DEPTH, 4, BR, ng, TN_GATES).transpose(0, 3, 1, 2, 4)
    bg = b_gate.reshape(DEPTH, 4, 1, ng, TN_GATES).transpose(0, 3, 1, 2, 4)
    wo = w_out.astype(bf16)
    fg = final_gain.reshape(1, D_MODEL)
    for l in range(DEPTH):
        lw = _layer_weights(l, p)
        gain = norm_gain[l].reshape(1, D_MODEL)
        start = 0
        for g in groups:
            B, L = g["B"], g["L"]
            mod = mod_all[l, start:start + B].reshape(B, 3, D_MODEL)
            start += B
            proj, small, h = _inproj(g["x"], mod, gain, w_in_p[l], w_in_s[l], L)
            branches = _mix(proj, small, lw, g["consts"], B, L)
            merged = _gated_merge(h, branches, wg[l], bg[l], wb[l], L)
            g["x"] = _outproj(merged, wo[l], g["x"], mod, fg, L, l == DEPTH - 1)
    return tuple(g["x"].reshape(g["B"], g["L"], D_MODEL) for g in groups)
```

```python
import functools
import math

import jax
import jax.numpy as jnp
from jax import lax
from jax.experimental import pallas as pl
from jax.experimental.pallas import tpu as pltpu

f32 = jnp.float32
bf16 = jnp.bfloat16

D_MODEL = 2048
DEPTH = 4
BR = 512
HEADS = 4
DH = 128
ROPE_BASE = 10000.0
GLA_DK = 64
GLA_KW = 256
GLA_RANK = 16
GLA_GATE_NORM = 16.0
HY_EMB = 33
HY_EMB_PAD = 40
HY_BANDS = 16
HY_HIDDEN = 64
HY_MIN_DECAY = math.log(1e-2) / 1.5
HY_MAX_DECAY = math.log(1e-2) / 0.3
ML_BLOCK = 4
EPS = 1e-6
NEG = -1e30

BLK_RET = 0
BLK_GLA_QK = 4
BLK_GLA_V = 5
BLK_GLA_G = 6
BLK_HY = 7
BLK_ML = 11
N_PROJ = 7168
N_SMALL = 128
LANE_GLA_A = 0
LANE_ML_I = 32
LANE_ML_F = 40

CHUNK = 128
HALO = 16
FFT_N2 = 128
FFT_K1T = 8
FFT_NJ = 4
PITCH_X = FFT_N2 + 8
PITCH_A = 2 * FFT_N2 + 8
TM_IN = 1024
TN_IN = 1792
TM_GATES = 1024
TN_GATES = 512
TM_OUT = 512
SEQ_PER_STEP = 2
VMEM_LIMIT = 60 * 1024 * 1024

LOG_GAMMA = tuple(math.log(1.0 - 2.0 ** (-5.0 - h)) for h in range(HEADS))


def _params(sem):
    return pltpu.CompilerParams(dimension_semantics=sem, vmem_limit_bytes=VMEM_LIMIT)


def _bdot(a, b):
    return jnp.dot(a.astype(bf16), b.astype(bf16), preferred_element_type=f32)


def _bdot_nt(a, b):
    return lax.dot_general(a.astype(bf16), b.astype(bf16), (((1,), (1,)), ((), ())),
                           preferred_element_type=f32)


def _bdot_tn(a, b):
    return jnp.dot(a.T.astype(bf16), b.astype(bf16), preferred_element_type=f32)


def _dot3(a, b):
    a1 = a.astype(bf16)
    ra = a - a1.astype(f32)
    a2 = ra.astype(bf16)
    a3 = (ra - a2.astype(f32)).astype(bf16)
    b1 = b.astype(bf16)
    rb = b - b1.astype(f32)
    b2 = rb.astype(bf16)
    b3 = (rb - b2.astype(f32)).astype(bf16)
    d = functools.partial(jnp.dot, preferred_element_type=f32)
    return (d(a1, b1) + (d(a1, b2) + d(a2, b1))) + ((d(a2, b2) + d(a1, b3)) + d(a3, b1))


def _cumsum01(tri, x):
    x1 = x.astype(bf16)
    r = x - x1.astype(f32)
    x2 = r.astype(bf16)
    x3 = (r - x2.astype(f32)).astype(bf16)
    d = functools.partial(jnp.dot, preferred_element_type=f32)
    return d(tri, x1) + (d(tri, x2) + d(tri, x3))


def _sigmoid(x):
    return jax.nn.sigmoid(x)


def _silu(x):
    return x * jax.nn.sigmoid(x)


def _log_sigmoid(x):
    return jnp.minimum(x, 0.0) - jnp.log1p(jnp.exp(-jnp.abs(x)))


def _tri(n, rev):
    ii = lax.broadcasted_iota(jnp.int32, (n, n), 0)
    jj = lax.broadcasted_iota(jnp.int32, (n, n), 1)
    return (jj >= ii) if rev else (jj <= ii)


def _prev_rows(x, first):
    row = lax.broadcasted_iota(jnp.int32, x.shape, 0)
    return jnp.where(row == 0, first, pltpu.roll(x, 1, 0))


def _next_rows(x, last):
    n = x.shape[0]
    row = lax.broadcasted_iota(jnp.int32, x.shape, 0)
    return jnp.where(row == n - 1, last, pltpu.roll(x, n - 1, 0))


def _conv3(x, prev, nxt, has_prev, has_next, w, b):
    first = jnp.where(has_prev, prev[HALO - 1:HALO, :], 0.0)
    last = jnp.where(has_next, nxt[0:1, :], 0.0)
    return _prev_rows(x, first) * w[0:1] + x * w[1:2] + _next_rows(x, last) * w[2:3] + b


def _head_norm(x, center):
    if center:
        x = x - jnp.mean(x, axis=-1, keepdims=True)
    return x * lax.rsqrt(jnp.mean(x * x, axis=-1, keepdims=True) + EPS)


def _ada_kernel(c_ref, w_ref, b_ref, o_ref):
    o_ref[0] = _bdot(_silu(c_ref[...]), w_ref[0]) + b_ref[0]


def _ada_all(c_all, w_ada, b_ada):
    rows = c_all.shape[0]
    tn = 1536
    return pl.pallas_call(
        _ada_kernel,
        grid=(DEPTH, 3 * D_MODEL // tn),
        in_specs=[
            pl.BlockSpec((rows, D_MODEL), lambda l, j: (0, 0)),
            pl.BlockSpec((1, D_MODEL, tn), lambda l, j: (l, 0, j)),
            pl.BlockSpec((1, 1, tn), lambda l, j: (l, 0, j)),
        ],
        out_specs=pl.BlockSpec((1, rows, tn), lambda l, j: (l, 0, j)),
        out_shape=jax.ShapeDtypeStruct((DEPTH, rows, 3 * D_MODEL), f32),
        compiler_params=_params(("arbitrary", "arbitrary")),
        name="ada",
    )(c_all, w_ada, b_ada.reshape(DEPTH, 1, 3 * D_MODEL))


ROWS_NORM = 256


def _modulated_norm(x, mod, gain):
    y = x * lax.rsqrt(jnp.mean(x * x, axis=-1, keepdims=True) + EPS) * gain
    return y * (1.0 + mod[1:2]) + mod[0:1]


def _norm_to_scratch(x_ref, mod_ref, gain_ref, h_ref):
    mod = mod_ref[0]
    gain = gain_ref[...]

    def step(i, carry):
        rows = pl.ds(pl.multiple_of(i * ROWS_NORM, ROWS_NORM), ROWS_NORM)
        h_ref[rows, :] = _modulated_norm(x_ref[rows, :], mod, gain).astype(bf16)
        return carry

    lax.fori_loop(0, x_ref.shape[0] // ROWS_NORM, step, 0)


def _inproj_kernel(x_ref, mod_ref, gain_ref, w_ref, ws_ref, o_ref, os_ref, h_ref):
    @pl.when(pl.program_id(1) == 0)
    def _():
        _norm_to_scratch(x_ref, mod_ref, gain_ref, h_ref)
        os_ref[...] = jnp.dot(h_ref[...], ws_ref[...], preferred_element_type=f32)

    o_ref[...] = jnp.dot(h_ref[...], w_ref[...], preferred_element_type=f32).astype(bf16)


def _inproj(x, mod, gain, w, ws, L):
    T = x.shape[0]
    tm = min(TM_IN, L)
    return pl.pallas_call(
        _inproj_kernel,
        grid=(T // tm, N_PROJ // TN_IN),
        in_specs=[
            pl.BlockSpec((tm, D_MODEL), lambda i, j: (i, 0)),
            pl.BlockSpec((1, 3, D_MODEL), lambda i, j: ((i * tm) // L, 0, 0)),
            pl.BlockSpec((1, D_MODEL), lambda i, j: (0, 0)),
            pl.BlockSpec((D_MODEL, TN_IN), lambda i, j: (0, j)),
            pl.BlockSpec((D_MODEL, N_SMALL), lambda i, j: (0, 0)),
        ],
        out_specs=[pl.BlockSpec((tm, TN_IN), lambda i, j: (i, j)),
                   pl.BlockSpec((tm, N_SMALL), lambda i, j: (i, 0)),
                   pl.BlockSpec((tm, D_MODEL), lambda i, j: (i, 0))],
        out_shape=[jax.ShapeDtypeStruct((T, N_PROJ), bf16), jax.ShapeDtypeStruct((T, N_SMALL), f32),
                   jax.ShapeDtypeStruct((T, D_MODEL), bf16)],
        compiler_params=_params(("arbitrary", "arbitrary")),
        name="inproj",
    )(x, mod, gain, w, ws)


def _gates_kernel(h_ref, b0_ref, b1_ref, b2_ref, b3_ref, wg_ref, bg_ref, wb_ref, o_ref):
    h = h_ref[...]
    merged = None
    for bi, br_ref in enumerate((b0_ref, b1_ref, b2_ref, b3_ref)):
        g = _sigmoid(jnp.dot(h, wg_ref[0, bi], preferred_element_type=f32) + bg_ref[0, bi])
        t = g * jnp.dot(br_ref[...], wb_ref[0, bi], preferred_element_type=f32)
        merged = t if merged is None else merged + t
    o_ref[...] = merged.astype(bf16)


def _gated_merge(h, branches, wg, bg, wb, L):
    T = h.shape[0]
    tm = min(TM_GATES, L)
    nj, _, _, tn = wg.shape
    br_spec = pl.BlockSpec((tm, BR), lambda i, j: (i, 0))
    return pl.pallas_call(
        _gates_kernel,
        grid=(T // tm, nj),
        in_specs=[
            pl.BlockSpec((tm, D_MODEL), lambda i, j: (i, 0)),
            br_spec, br_spec, br_spec, br_spec,
            pl.BlockSpec((1, 4, D_MODEL, tn), lambda i, j: (j, 0, 0, 0)),
            pl.BlockSpec((1, 4, 1, tn), lambda i, j: (j, 0, 0, 0)),
            pl.BlockSpec((1, 4, BR, tn), lambda i, j: (j, 0, 0, 0)),
        ],
        out_specs=pl.BlockSpec((tm, tn), lambda i, j: (i, j)),
        out_shape=jax.ShapeDtypeStruct((T, D_MODEL), bf16),
        compiler_params=_params(("arbitrary", "arbitrary")),
        name="gates",
    )(h, *branches, wg, bg, wb)


def _outproj_kernel(final, m_ref, wo_ref, x_ref, mod_ref, fg_ref, o_ref):
    y = x_ref[...] + mod_ref[0][2:3] * jnp.dot(m_ref[...], wo_ref[...], preferred_element_type=f32)
    if final:
        y = y * lax.rsqrt(jnp.mean(y * y, axis=-1, keepdims=True) + EPS) * fg_ref[...]
    o_ref[...] = y


def _outproj(merged, wo, x, mod, final_gain, L, final):
    T = x.shape[0]
    tm = min(TM_OUT, L)
    tile = pl.BlockSpec((tm, D_MODEL), lambda i: (i, 0))
    return pl.pallas_call(
        functools.partial(_outproj_kernel, final),
        grid=(T // tm,),
        in_specs=[tile,
                  pl.BlockSpec((D_MODEL, D_MODEL), lambda i: (0, 0), pipeline_mode=pl.Buffered(1)),
                  tile,
                  pl.BlockSpec((1, 3, D_MODEL), lambda i: ((i * tm) // L, 0, 0)),
                  pl.BlockSpec((1, D_MODEL), lambda i: (0, 0))],
        out_specs=tile,
        out_shape=jax.ShapeDtypeStruct((T, D_MODEL), f32),
        compiler_params=_params(("arbitrary",)),
        name="outproj",
    )(merged, wo, x, mod, final_gain)


def _chunk_idx(nc, rev):
    if rev:
        return lambda c: nc - 1 - c
    return lambda c: c


def _tok_spec(ns, nc, rev, blk, width=BR):
    idx = _chunk_idx(nc, rev)
    return pl.BlockSpec((ns, CHUNK, width), lambda b, c: (b, idx(c), blk))


def _per_sequence(body, batched, *refs):
    ns = [r for r, f in zip(refs, batched) if f][0].shape[0]
    for s in range(ns):
        body(*[r.at[s] if f else r for r, f in zip(refs, batched)])


def _seqs_per_step(B):
    return SEQ_PER_STEP if B % SEQ_PER_STEP == 0 else 1


def _pos_spec(nc, rev):
    if rev:
        return pl.BlockSpec((CHUNK, DH), lambda b, c: (nc - 1 - c, 0))
    return pl.BlockSpec((CHUNK, DH), lambda b, c: (c, 0))


def _const_spec(shape):
    nd = len(shape)
    return pl.BlockSpec(shape, lambda b, c: (0,) * nd)


def _ret_kernel(rev, final, batched, *refs):
    _per_sequence(functools.partial(_ret_body, rev, final), batched, *refs)


def _ret_body(rev, final, *refs):
    if final:
        q_ref, k_ref, v_ref, cos_ref, sin_ref, g_ref, rb_ref, o_ref, s_ref = refs
    else:
        q_ref, k_ref, v_ref, cos_ref, sin_ref, o_ref, s_ref = refs
    C = CHUNK

    @pl.when(pl.program_id(1) == 0)
    def _():
        s_ref[...] = jnp.zeros_like(s_ref)

    cos = cos_ref[...]
    sin = sin_ref[...]
    ii = lax.broadcasted_iota(jnp.int32, (C, C), 0)
    jj = lax.broadcasted_iota(jnp.int32, (C, C), 1)
    col = lax.broadcasted_iota(jnp.int32, (C, 1), 0).astype(f32)
    dist = (ii - jj).astype(f32)
    for h in range(HEADS):
        lg = LOG_GAMMA[h]
        sl = slice(h * DH, (h + 1) * DH)
        q = q_ref[:, sl].astype(f32)
        k = k_ref[:, sl].astype(f32)
        v = v_ref[:, sl]
        q = q * cos + pltpu.roll(q, DH // 2, 1) * sin
        k = (k * cos + pltpu.roll(k, DH // 2, 1) * sin) * (DH ** -0.5)
        if rev:
            gam = jnp.where(jj > ii, jnp.exp(-dist * lg), 0.0)
            qd = jnp.exp((C - col) * lg)
            kd = jnp.exp(col * lg)
        else:
            gam = jnp.where(jj <= ii, jnp.exp(dist * lg), 0.0)
            qd = jnp.exp((col + 1.0) * lg)
            kd = jnp.exp((C - 1.0 - col) * lg)
        s_prev = s_ref[h]
        half = C // 2
        o = jnp.concatenate([
            _bdot(_bdot_nt(q[r:r + half], k) * gam[r:r + half], v) + qd[r:r + half] * _bdot(q[r:r + half], s_prev)
            for r in (0, half)], axis=0)
        s_ref[h] = math.exp(C * lg) * s_prev + _bdot_tn(k * kd, v)
        if final:
            r = _head_norm(o + rb_ref[:, sl], center=True)
            o_ref[:, sl] = (r * _silu(g_ref[:, sl].astype(f32))).astype(o_ref.dtype)
        else:
            o_ref[:, sl] = o


def _ret_branch(proj, cosf, sins, B, L):
    nc = L // CHUNK
    ns = _seqs_per_step(B)

    def call(rev, final, extra):
        in_specs = [_tok_spec(ns, nc, rev, BLK_RET + i) for i in range(3)]
        in_specs += [_pos_spec(nc, rev), _pos_spec(nc, rev)]
        args = [proj, proj, proj, cosf, sins]
        batched = [True, True, True, False, False]
        if final:
            in_specs += [_tok_spec(ns, nc, rev, BLK_RET + 3), _tok_spec(ns, nc, rev, 0)]
            args += [proj, extra]
            batched += [True, True]
        batched += [True, True]
        return pl.pallas_call(
            functools.partial(_ret_kernel, rev, final, tuple(batched)),
            grid=(B // ns, nc),
            in_specs=in_specs,
            out_specs=_tok_spec(ns, nc, rev, 0),
            out_shape=jax.ShapeDtypeStruct((B, L, BR), bf16 if final else f32),
            scratch_shapes=[pltpu.VMEM((ns, HEADS, DH, DH), f32)],
            compiler_params=_params(("arbitrary", "arbitrary")),
            name="ret_fwd" if final else "ret_bwd",
        )(*args)

    rb = call(True, False, None)
    return call(False, True, rb)


def _gla_kernel(rev, final, batched, *refs):
    _per_sequence(functools.partial(_gla_body, rev, final), batched, *refs)


def _gla_body(rev, final, *refs):
    if final:
        qk_ref, v_ref, small_ref, wa_ref, ba_ref, g_ref, ob_ref, ng_ref, o_ref, st_ref = refs
    else:
        qk_ref, v_ref, small_ref, wa_ref, ba_ref, o_ref, st_ref = refs
    C = CHUNK

    @pl.when(pl.program_id(1) == 0)
    def _():
        st_ref[...] = jnp.zeros_like(st_ref)

    q = qk_ref[:, :GLA_KW].astype(f32) * (GLA_DK ** -0.5)
    k = qk_ref[:, GLA_KW:].astype(f32)
    v = v_ref[...].astype(f32)
    tri = _tri(C, rev)
    log_a = _log_sigmoid(_bdot(small_ref[...], wa_ref[...]) + ba_ref[...]) / GLA_GATE_NORM
    b = _cumsum01(tri.astype(bf16), log_a)
    ref_row = b[C // 2:C // 2 + 1, :]
    b_end = b[0:1, :] if rev else b[C - 1:C, :]
    qt = q * jnp.exp(b - ref_row)
    kt = k * jnp.exp(ref_row - b)
    inter = _bdot_nt(q * jnp.exp(b), st_ref[...])
    lane = lax.broadcasted_iota(jnp.int32, (1, GLA_KW), 1)
    for h in range(HEADS):
        sl = slice(h * DH, (h + 1) * DH)
        a = _bdot_nt(jnp.where(lane // GLA_DK == h, qt, 0.0), kt)
        o = _bdot(jnp.where(tri, a, 0.0), v[:, sl]) + inter[:, sl]
        if final:
            o = _head_norm(o + ob_ref[:, sl], center=False) * ng_ref[...]
            o_ref[:, sl] = (o * _silu(g_ref[:, sl].astype(f32))).astype(o_ref.dtype)
        else:
            o_ref[:, sl] = o
    ee = lax.broadcasted_iota(jnp.int32, (BR, GLA_KW), 0) // DH
    dd = lax.broadcasted_iota(jnp.int32, (BR, GLA_KW), 1) // GLA_DK
    upd = _bdot_tn(v, k * jnp.exp(b_end - b))
    st_ref[...] = st_ref[...] * jnp.exp(b_end) + jnp.where(ee == dd, upd, 0.0)


def _gla_branch(proj, small, wa_pad, ba, norm_gain, B, L):
    nc = L // CHUNK
    ns = _seqs_per_step(B)

    def call(rev, final, extra):
        d = 1 if rev else 0
        in_specs = [_tok_spec(ns, nc, rev, BLK_GLA_QK), _tok_spec(ns, nc, rev, BLK_GLA_V),
                    _tok_spec(ns, nc, rev, 0, N_SMALL),
                    _const_spec((128, GLA_KW)), _const_spec((1, GLA_KW))]
        args = [proj, proj, small, wa_pad[d], ba[d:d + 1]]
        batched = [True, True, True, False, False]
        if final:
            in_specs += [_tok_spec(ns, nc, rev, BLK_GLA_G), _tok_spec(ns, nc, rev, 0), _const_spec((1, DH))]
            args += [proj, extra, norm_gain]
            batched += [True, True, False]
        batched += [True, True]
        return pl.pallas_call(
            functools.partial(_gla_kernel, rev, final, tuple(batched)),
            grid=(B // ns, nc),
            in_specs=in_specs,
            out_specs=_tok_spec(ns, nc, rev, 0),
            out_shape=jax.ShapeDtypeStruct((B, L, BR), bf16 if final else f32),
            scratch_shapes=[pltpu.VMEM((ns, BR, GLA_KW), f32)],
            compiler_params=_params(("arbitrary", "arbitrary")),
            name="gla_fwd" if final else "gla_bwd",
        )(*args)

    ob = call(True, False, None)
    return call(False, True, ob)


def _ml_kernel(rev, final, batched, *refs):
    _per_sequence(functools.partial(_ml_body, rev, final), batched, *refs)


def _ml_body(rev, final, *refs):
    if final:
        (u_ref, up_ref, un_ref, small_ref, cw_ref, cb_ref, wq_ref, wk_ref, wv_ref, bias_ref,
         zg_ref, op_ref, hb_ref, ng_ref, skip_ref, o_ref, s_ref, n_ref, m_ref) = refs
    else:
        (u_ref, up_ref, un_ref, small_ref, cw_ref, cb_ref, wq_ref, wk_ref, wv_ref, bias_ref,
         o_ref, s_ref, n_ref, m_ref) = refs
    C = CHUNK
    c = pl.program_id(1)
    nc = pl.num_programs(1)

    @pl.when(c == 0)
    def _():
        s_ref[...] = jnp.zeros_like(s_ref)
        n_ref[...] = jnp.zeros_like(n_ref)
        m_ref[...] = jnp.full(m_ref.shape, NEG, f32)

    pos = (nc - 1 - c) if rev else c
    u = u_ref[...].astype(f32)
    xc = _silu(_conv3(u, up_ref[...].astype(f32), un_ref[...].astype(f32), pos > 0, pos < nc - 1,
                      cw_ref[...], cb_ref[...]))
    q = _bdot(xc, wq_ref[...])
    k = _bdot(xc, wk_ref[...]) * (DH ** -0.5)
    v = _bdot(u, wv_ref[...])

    gates = small_ref[...] + bias_ref[...]
    log_f = _log_sigmoid(gates)
    tri = _tri(C, rev)
    cum = _cumsum01(tri.astype(bf16), log_f)
    gates_t = gates.T
    cum_t = cum.T
    d = 1 if rev else 0
    for h in range(HEADS):
        sl = slice(h * DH, (h + 1) * DH)
        li = LANE_ML_I + 4 * d + h
        lf = LANE_ML_F + 4 * d + h
        b_col = cum[:, lf:lf + 1]
        b_row = cum_t[lf:lf + 1, :]
        i_col = gates[:, li:li + 1]
        i_row = gates_t[li:li + 1, :]
        b_end = b_col[0:1, :] if rev else b_col[C - 1:C, :]
        m_prev = m_ref[h][0:1, 0:1]
        s_prev = s_ref[h]
        n_prev = n_ref[h][0:1, :]
        qh = q[:, sl]
        kh = k[:, sl]
        vh = v[:, sl]
        dmat = jnp.where(tri, b_col - b_row + i_row, NEG)
        m_in = b_col + m_prev
        m_t = jnp.maximum(m_in, jnp.max(dmat, axis=1, keepdims=True))
        p = jnp.exp(dmat - m_t) * _bdot_nt(qh, kh)
        a_in = jnp.exp(m_in - m_t)
        num = _bdot(p, vh) + a_in * _bdot(qh, s_prev)
        den = jnp.sum(p, axis=1, keepdims=True) + a_in * jnp.sum(qh * n_prev, axis=1, keepdims=True)
        hd = num / jnp.maximum(jnp.abs(den), jnp.exp(-m_t))
        g_col = b_end - b_col + i_col
        m_loc = jnp.max(g_col, axis=0, keepdims=True)
        kw = kh * jnp.exp(g_col - m_loc)
        m_new = jnp.maximum(b_end + m_prev, m_loc)
        a_old = jnp.exp(b_end + m_prev - m_new)
        a_new = jnp.exp(m_loc - m_new)
        s_ref[h] = a_old * s_prev + a_new * _bdot_tn(kw, vh)
        n_ref[h] = jnp.broadcast_to(a_old * n_prev + a_new * jnp.sum(kw, axis=0, keepdims=True), (8, DH))
        m_ref[h] = jnp.broadcast_to(m_new, (8, DH))
        if final:
            hh = _sigmoid(op_ref[:, sl].astype(f32)) * (hd + hb_ref[:, sl])
            hh = _head_norm(hh, center=True) * ng_ref[:, sl] + skip_ref[:, sl] * xc[:, sl]
            o_ref[:, sl] = (hh * _silu(zg_ref[:, sl].astype(f32))).astype(o_ref.dtype)
        else:
            o_ref[:, sl] = hd


def _ml_branch(proj, small, conv_w, conv_b, wq, wk, wv, bias_small, norm_gain, skip, B, L):
    nc = L // CHUNK
    ns = _seqs_per_step(B)
    per = CHUNK // HALO
    last = L // HALO - 1

    def call(rev, final, extra):
        idx = _chunk_idx(nc, rev)
        prev_spec = pl.BlockSpec((ns, HALO, BR), lambda b, c: (b, jnp.maximum(idx(c) * per - 1, 0), BLK_ML))
        next_spec = pl.BlockSpec((ns, HALO, BR), lambda b, c: (b, jnp.minimum((idx(c) + 1) * per, last), BLK_ML))
        in_specs = [_tok_spec(ns, nc, rev, BLK_ML), prev_spec, next_spec, _tok_spec(ns, nc, rev, 0, N_SMALL),
                    _const_spec((3, BR)), _const_spec((1, BR)),
                    _const_spec((BR, BR)), _const_spec((BR, BR)), _const_spec((BR, BR)),
                    _const_spec((1, 128))]
        args = [proj, proj, proj, small, conv_w, conv_b, wq, wk, wv, bias_small]
        batched = [True] * 4 + [False] * 6
        if final:
            in_specs += [_tok_spec(ns, nc, rev, BLK_ML + 1), _tok_spec(ns, nc, rev, BLK_ML + 2),
                         _tok_spec(ns, nc, rev, 0), _const_spec((1, BR)), _const_spec((1, BR))]
            args += [proj, proj, extra, norm_gain, skip]
            batched += [True, True, True, False, False]
        batched += [True] * 4
        return pl.pallas_call(
            functools.partial(_ml_kernel, rev, final, tuple(batched)),
            grid=(B // ns, nc),
            in_specs=in_specs,
            out_specs=_tok_spec(ns, nc, rev, 0),
            out_shape=jax.ShapeDtypeStruct((B, L, BR), bf16 if final else f32),
            scratch_shapes=[pltpu.VMEM((ns, HEADS, DH, DH), f32), pltpu.VMEM((ns, HEADS, 8, DH), f32),
                            pltpu.VMEM((ns, HEADS, 8, DH), f32)],
            compiler_params=_params(("arbitrary", "arbitrary")),
            name="ml_fwd" if final else "ml_bwd",
        )(*args)

    hb = call(True, False, None)
    return call(False, True, hb)


def _hy_filt_kernel(emb_ref, w1_ref, b1_ref, w2_ref, b2_ref, fr_ref, w3_ref, dl_ref, taps_ref, sc_ref):
    r = pl.program_id(0)

    @pl.when(r == 0)
    def _():
        sc_ref[...] = jnp.zeros_like(sc_ref)

    emb = emb_ref[...]
    fr = fr_ref[...]
    hmid = jnp.sin(fr * (_dot3(emb, w1_ref[...]) + b1_ref[...]))
    hmid = jnp.sin(fr * (_dot3(hmid, w2_ref[...]) + b2_ref[...]))
    hout = _dot3(hmid, w3_ref[0])
    t = emb[:, 0:1]
    valid = emb[:, HY_EMB:HY_EMB + 1]
    taps = hout * jnp.exp(-t * dl_ref[...]) * valid
    taps_ref[...] = taps.astype(taps_ref.dtype)
    sc_ref[...] += jnp.sum(taps * taps, axis=0, keepdims=True)

    @pl.when(r == pl.num_programs(0) - 1)
    def _():
        sc_ref[...] = lax.rsqrt(sc_ref[...] + EPS)


def _hy_filter(emb, w1p, b1, w2, b2, freq, w3d, deltas2, L):
    n = 2 * L
    tr = 512
    half = L // tr
    return pl.pallas_call(
        _hy_filt_kernel,
        grid=(n // tr,),
        in_specs=[
            pl.BlockSpec((tr, HY_EMB_PAD), lambda r: (r, 0)),
            pl.BlockSpec((HY_EMB_PAD, HY_HIDDEN), lambda r: (0, 0)),
            pl.BlockSpec((1, HY_HIDDEN), lambda r: (0, 0)),
            pl.BlockSpec((HY_HIDDEN, HY_HIDDEN), lambda r: (0, 0)),
            pl.BlockSpec((1, HY_HIDDEN), lambda r: (0, 0)),
            pl.BlockSpec((1, HY_HIDDEN), lambda r: (0, 0)),
            pl.BlockSpec((1, HY_HIDDEN, 2 * BR), lambda r: (r // half, 0, 0)),
            pl.BlockSpec((1, 2 * BR), lambda r: (0, 0)),
        ],
        out_specs=[pl.BlockSpec((tr, 2 * BR), lambda r: (r, 0)),
                   pl.BlockSpec((1, 2 * BR), lambda r: (0, 0))],
        out_shape=[jax.ShapeDtypeStruct((n, 2 * BR), bf16), jax.ShapeDtypeStruct((1, 2 * BR), f32)],
        compiler_params=_params(("arbitrary",)),
        name="hy_filter",
    )(emb, w1p, b1, w2, b2, freq, w3d, deltas2)


def _seq_chunk(ref, i, nchunks, w, b):
    r0 = pl.multiple_of(i * FFT_N2, FFT_N2)
    x = ref[pl.ds(r0, FFT_N2), :].astype(f32)
    if w is None:
        return x
    total = nchunks * FFT_N2
    p0 = pl.multiple_of(jnp.maximum(r0 - HALO, 0), HALO)
    n0 = pl.multiple_of(jnp.minimum(r0 + FFT_N2, total - HALO), HALO)
    prev = ref[pl.ds(p0, HALO), :].astype(f32)
    nxt = ref[pl.ds(n0, HALO), :].astype(f32)
    return _conv3(x, prev, nxt, i > 0, i < nchunks - 1, w, b)


def _dft_outer_kernel(K, k1t, conv, *refs):
    if conv:
        x_ref, cw_ref, cb_ref, f_ref, o_ref, xs_ref = refs
        w, b = cw_ref[0], cb_ref[0]
    else:
        x_ref, f_ref, o_ref, xs_ref = refs
        w = b = None

    @pl.when(pl.program_id(2) == 0)
    def _():
        def fill(i, carry):
            xs_ref[pl.ds(pl.multiple_of(i * PITCH_X, 8), FFT_N2), :] = _seq_chunk(x_ref, i, K, w, b)
            return carry
        lax.fori_loop(0, K, fill, 0)

    f = f_ref[0]

    def body(g, carry):
        j0 = g * FFT_NJ
        x = jnp.concatenate([xs_ref[pl.ds(j0 + i, K, stride=PITCH_X), :] for i in range(FFT_NJ)], axis=1)
        r = jnp.dot(f, x.astype(bf16), preferred_element_type=f32)
        for i in range(FFT_NJ):
            sl = slice(i * 128, (i + 1) * 128)
            o_ref[pl.ds(j0 + i, k1t, stride=PITCH_A), :] = r[:k1t, sl]
            o_ref[pl.ds(FFT_N2 + j0 + i, k1t, stride=PITCH_A), :] = r[k1t:, sl]
        return carry

    lax.fori_loop(0, FFT_N2 // FFT_NJ, body, 0, unroll=2)
    zero = jnp.zeros((k1t, 128), f32)
    for i in range(PITCH_A - 2 * FFT_N2):
        o_ref[pl.ds(2 * FFT_N2 + i, k1t, stride=PITCH_A), :] = zero


def _dft_outer(x, col0, ncol, fmat, B, K, K1p, k1t, conv=None):
    nt = K1p // k1t
    in_specs = [pl.BlockSpec((K * FFT_N2, 128), lambda b, c, t: (b, col0 + c))]
    args = [x]
    if conv is not None:
        cw, cb, s = conv
        in_specs += [pl.BlockSpec((1, 3, 128), lambda b, c, t: (s, 0, c)),
                     pl.BlockSpec((1, 1, 128), lambda b, c, t: (s, 0, c))]
        args += [cw, cb]
    in_specs.append(pl.BlockSpec((1, 2 * k1t, K), lambda b, c, t: (t, 0, 0)))
    args.append(fmat)
    return pl.pallas_call(
        functools.partial(_dft_outer_kernel, K, k1t, conv is not None),
        grid=(B, ncol, nt),
        in_specs=in_specs,
        out_specs=pl.BlockSpec((k1t * PITCH_A, 128), lambda b, c, t: (b * nt + t, c)),
        out_shape=jax.ShapeDtypeStruct((B * K1p * PITCH_A, 128 * ncol), f32),
        scratch_shapes=[pltpu.VMEM((K * PITCH_X, 128), f32)],
        compiler_params=_params(("arbitrary", "arbitrary", "arbitrary")),
        name="dft_outer",
    )(*args)


def _dft_inner_kernel(conv, *refs):
    if conv:
        a_ref, gf_ref, h_ref, gi_ref, o_ref = refs
    else:
        a_ref, gf_ref, o_ref = refs
    n2 = FFT_N2
    dot = functools.partial(jnp.dot, preferred_element_type=f32)
    for kk in range(FFT_K1T):
        gf = gf_ref[kk]
        x = dot(gf[:, :n2], a_ref[0, kk, :n2].astype(bf16)) + dot(gf[:, n2:], a_ref[0, kk, n2:2 * n2].astype(bf16))
        if conv:
            xr, xi = x[:n2], x[n2:]
            hr, hi = h_ref[0, kk, :n2], h_ref[0, kk, n2:2 * n2]
            yr = (xr * hr - xi * hi).astype(bf16)
            yi = (xr * hi + xi * hr).astype(bf16)
            gi = gi_ref[kk]
            x = dot(gi[:, :n2], yr) + dot(gi[:, n2:], yi)
        o_ref[0, kk, :2 * n2] = x
        o_ref[0, kk, 2 * n2:] = jnp.zeros((PITCH_A - 2 * n2, BR), f32)


def _dft_inner(a4, gf, h4=None, gi=None, hblk=0):
    B, k1p, pa, W = a4.shape
    conv = h4 is not None
    kt = FFT_K1T
    r2 = 2 * FFT_N2
    in_specs = [pl.BlockSpec((1, kt, pa, BR), lambda k, w, b: (b, k, 0, w)),
                pl.BlockSpec((kt, r2, r2), lambda k, w, b: (k, 0, 0))]
    args = [a4, gf]
    if conv:
        in_specs += [pl.BlockSpec((1, kt, pa, BR), lambda k, w, b: (0, k, 0, hblk)),
                     pl.BlockSpec((kt, r2, r2), lambda k, w, b: (k, 0, 0))]
        args += [h4, gi]
    return pl.pallas_call(
        functools.partial(_dft_inner_kernel, conv),
        grid=(k1p // kt, W // BR, B),
        in_specs=in_specs,
        out_specs=pl.BlockSpec((1, kt, pa, BR), lambda k, w, b: (b, k, 0, w)),
        out_shape=jax.ShapeDtypeStruct((B, k1p, pa, W), f32),
        compiler_params=_params(("arbitrary", "arbitrary", "arbitrary")),
        name="dft_inner_conv" if conv else "dft_inner_spec",
    )(*args)


def _idft_outer_kernel(kh, K1p, final, conv_s, *refs):
    it = iter(refs)
    b_ref, fr_ref, fi_ref, mul_ref, s_ref, sc_ref, sk_ref, cwm_ref, cbm_ref = [next(it) for _ in range(9)]
    cws_ref, cbs_ref = (next(it), next(it)) if conv_s else (None, None)
    g_ref = next(it) if final else None
    o_ref, ys_ref = next(it), next(it)
    fr = fr_ref[...]
    fi = fi_ref[...]
    dot = functools.partial(jnp.dot, preferred_element_type=f32)

    def body(g, carry):
        j0 = g * FFT_NJ
        br = jnp.concatenate([b_ref[pl.ds(j0 + i, K1p, stride=PITCH_A), :] for i in range(FFT_NJ)], axis=1)
        bi = jnp.concatenate([b_ref[pl.ds(FFT_N2 + j0 + i, K1p, stride=PITCH_A), :] for i in range(FFT_NJ)], axis=1)
        pad = fr.shape[1] - K1p
        if pad:
            zero = jnp.zeros((pad, br.shape[1]), f32)
            br = jnp.concatenate([br, zero], axis=0)
            bi = jnp.concatenate([bi, zero], axis=0)
        y = dot(fr, br.astype(bf16)) + dot(fi, bi.astype(bf16))
        for i in range(FFT_NJ):
            ys_ref[pl.ds(j0 + i, kh, stride=PITCH_X), :] = y[:, i * 128:(i + 1) * 128]
        return carry

    lax.fori_loop(0, FFT_N2 // FFT_NJ, body, 0, unroll=2)
    sc = sc_ref[...]
    sk = sk_ref[...]

    def epi(i, carry):
        y = ys_ref[pl.ds(pl.multiple_of(i * PITCH_X, 8), FFT_N2), :]
        m = _seq_chunk(mul_ref, i, kh, cwm_ref[0], cbm_ref[0])
        s = _seq_chunk(s_ref, i, kh, cws_ref[0], cbs_ref[0]) if conv_s else _seq_chunk(s_ref, i, kh, None, None)
        z = m * (y * sc + s * sk)
        if final:
            z = z * _silu(_seq_chunk(g_ref, i, kh, None, None))
        o_ref[pl.ds(pl.multiple_of(i * FFT_N2, FFT_N2), FFT_N2), :] = z.astype(o_ref.dtype)
        return carry

    lax.fori_loop(0, kh, epi, 0)


def _idft_outer(b2, fr, fi, mul, s, scale, skip, conv_w, conv_b, B, kh, K1p, g=None):
    L = kh * FFT_N2
    once = pl.Buffered(1) if L > 4096 else None
    mul_a, mul_c, mul_s = mul
    s_a, s_c, s_s = s
    conv_s = s_s is not None
    vec = pl.BlockSpec((1, 128), lambda b, c: (0, c))
    in_specs = [pl.BlockSpec((K1p * PITCH_A, 128), lambda b, c: (b, c), pipeline_mode=once),
                pl.BlockSpec(fr.shape, lambda b, c: (0, 0)), pl.BlockSpec(fi.shape, lambda b, c: (0, 0)),
                pl.BlockSpec((L, 128), lambda b, c: (b, mul_c + c), pipeline_mode=once),
                pl.BlockSpec((L, 128), lambda b, c: (b, s_c + c), pipeline_mode=once),
                vec, vec,
                pl.BlockSpec((1, 3, 128), lambda b, c: (mul_s, 0, c)),
                pl.BlockSpec((1, 1, 128), lambda b, c: (mul_s, 0, c))]
    args = [b2, fr, fi, mul_a, s_a, scale, skip, conv_w, conv_b]
    if conv_s:
        in_specs += [pl.BlockSpec((1, 3, 128), lambda b, c: (s_s, 0, c)),
                     pl.BlockSpec((1, 1, 128), lambda b, c: (s_s, 0, c))]
        args += [conv_w, conv_b]
    if g is not None:
        g_a, g_c = g
        in_specs.append(pl.BlockSpec((L, 128), lambda b, c: (b, g_c + c), pipeline_mode=once))
        args.append(g_a)
    return pl.pallas_call(
        functools.partial(_idft_outer_kernel, kh, K1p, g is not None, conv_s),
        grid=(B, BR // 128),
        in_specs=in_specs,
        out_specs=pl.BlockSpec((L, 128), lambda b, c: (b, c)),
        out_shape=jax.ShapeDtypeStruct((B * L, BR), bf16),
        scratch_shapes=[pltpu.VMEM((kh * PITCH_X, 128), f32)],
        compiler_params=_params(("arbitrary", "arbitrary")),
        name="idft_outer",
    )(*args)


def _fft_tables(L):
    n = 2 * L
    n2 = FFT_N2
    n1 = n // n2
    kh = n1 // 2
    k1t = 48 if kh + 1 > 48 else -(-(kh + 1) // 8) * 8
    k1p = -(-(kh + 1) // k1t) * k1t
    nt = k1p // k1t
    two_pi = 2.0 * math.pi
    k1 = jnp.arange(k1p, dtype=jnp.int32)

    def outer(K):
        m = jnp.arange(K, dtype=jnp.int32)
        ang = (two_pi / n1) * ((k1[:, None] * m[None, :]) % n1).astype(f32)
        re = jnp.cos(ang).reshape(nt, k1t, K)
        im = (-jnp.sin(ang)).reshape(nt, k1t, K)
        return jnp.concatenate([re, im], axis=1).astype(bf16)

    r1 = jnp.arange(kh, dtype=jnp.int32)
    wgt = jnp.where((k1 == 0) | (k1 == kh), 1.0, jnp.where(k1 < kh, 2.0, 0.0)).astype(f32)
    angi = (two_pi / n1) * ((r1[:, None] * k1[None, :]) % n1).astype(f32)
    kpad = ((0, 0), (0, -k1p % 16))
    fi_re = jnp.pad(jnp.cos(angi) * wgt[None, :], kpad).astype(bf16)
    fi_im = jnp.pad(-jnp.sin(angi) * wgt[None, :], kpad).astype(bf16)
    kk = k1[:, None, None]
    a = jnp.arange(n2, dtype=jnp.int32)[None, :, None]
    c = jnp.arange(n2, dtype=jnp.int32)[None, None, :]
    angp = (two_pi / n) * ((a * c * n1 + c * kk) % n).astype(f32)
    mr, mi = jnp.cos(angp), -jnp.sin(angp)
    g_fwd = jnp.concatenate([jnp.concatenate([mr, -mi], axis=2),
                             jnp.concatenate([mi, mr], axis=2)], axis=1).astype(bf16)
    angq = (two_pi / n) * ((a * c * n1 + a * kk) % n).astype(f32)
    vr, vi = jnp.cos(angq), jnp.sin(angq)
    g_inv = jnp.concatenate([jnp.concatenate([vr, -vi], axis=2),
                             jnp.concatenate([vi, vr], axis=2)], axis=1).astype(bf16)
    return dict(n=n, n1=n1, kh=kh, k1p=k1p, k1t=k1t, f_half=outer(kh), f_full=outer(n1),
                fi_re=fi_re, fi_im=fi_im, g_fwd=g_fwd, g_inv=g_inv)


def _hy_embedding(L):
    n = jnp.arange(2 * L, dtype=jnp.int32)
    pos = jnp.where(n < L, n, 2 * L - n).astype(f32)
    t = pos / (L - 1)
    f = jnp.linspace(1e-4, HY_BANDS - 1, HY_BANDS, dtype=f32)
    ang = (2.0 * math.pi / L) * pos[:, None] * f[None, :]
    valid = (n != L).astype(f32)
    pad = jnp.zeros((2 * L, HY_EMB_PAD - HY_EMB - 1), f32)
    return jnp.concatenate([t[:, None], jnp.cos(ang), -jnp.sin(ang), valid[:, None], pad], axis=-1)


def _hy_branch(proj, lw, emb, tabs, B, L):
    n1, kh, k1p, k1t = tabs["n1"], tabs["kh"], tabs["k1p"], tabs["k1t"]
    cw, cb = lw["hy_conv_w"], lw["hy_conv_b"]
    taps, scale = _hy_filter(emb, lw["hy_w1"], lw["hy_b1"], lw["hy_w2"], lw["hy_b2"], lw["hy_freq"],
                             lw["hy_w3"], lw["hy_deltas"], L)
    ta = _dft_outer(taps, 0, 2 * BR // 128, tabs["f_full"], 1, n1, k1p, k1t)
    h4 = _dft_inner(ta.reshape(1, k1p, PITCH_A, 2 * BR), tabs["g_fwd"])
    scale = scale * (1.0 / tabs["n"])
    col = lambda s: (BLK_HY + s) * (BR // 128)

    def conv(x, col0, order, stream):
        a = _dft_outer(x, col0, BR // 128, tabs["f_half"], B, kh, k1p, k1t,
                       None if stream is None else (cw, cb, stream))
        b4 = _dft_inner(a.reshape(B, k1p, PITCH_A, BR), tabs["g_fwd"], h4, tabs["g_inv"], order)
        return b4.reshape(B * k1p * PITCH_A, BR)

    def inv(b2, order, mul, s, g=None):
        return _idft_outer(b2, tabs["fi_re"], tabs["fi_im"], mul, s, scale[:, order * BR:(order + 1) * BR],
                           lw["hy_skip"][order:order + 1], cw, cb, B, kh, k1p, g)

    z = inv(conv(proj, col(0), 0, 0), 0, (proj, col(1), 1), (proj, col(0), 0))
    return inv(conv(z, 0, 1, None), 1, (proj, col(2), 2), (z, 0, None), (proj, col(3)))


def _rotary_tables(L):
    inv = ROPE_BASE ** (-jnp.arange(0, DH, 2, dtype=f32) / DH)
    ang = jnp.arange(L, dtype=f32)[:, None] * inv[None, :]
    cos, sin = jnp.cos(ang), jnp.sin(ang)
    return jnp.concatenate([cos, cos], axis=-1), jnp.concatenate([-sin, sin], axis=-1)


def _block_diag(w):
    n, e, fo = w.shape
    eye = jnp.eye(n, dtype=w.dtype)
    return (eye[:, None, :, None] * w[:, :, None, :]).reshape(n * e, n * fo)


def _permute_w_in(w_in):
    ref_gla_small = 2048 + 1536
    ref_hy = ref_gla_small + 2 * GLA_RANK
    ref_ml = ref_hy + 2048
    ref_ml_small = ref_ml + 1536
    main = jnp.concatenate([w_in[..., :ref_gla_small], w_in[..., ref_hy:ref_ml], w_in[..., ref_ml:ref_ml_small]],
                           axis=-1)
    small = jnp.concatenate([w_in[..., ref_gla_small:ref_hy], w_in[..., ref_ml_small:]], axis=-1)
    small = jnp.pad(small, [(0, 0)] * (small.ndim - 1) + [(0, N_SMALL - small.shape[-1])])
    return main.astype(bf16), small.astype(bf16)


def _layer_weights(l, p):
    lw = {}
    lw["gla_wa"] = jnp.stack([
        jnp.pad(p["gla_w_a"][l, d], ((LANE_GLA_A + GLA_RANK * d, 128 - LANE_GLA_A - GLA_RANK * (d + 1)), (0, 0)))
        for d in range(2)]).astype(bf16)
    lw["gla_ba"] = p["gla_b_a"][l]
    lw["gla_ng"] = p["gla_norm_gain"][l].reshape(1, DH)
    lw["hy_conv_w"] = p["hy_conv_w"][l].reshape(3, 3, BR).transpose(1, 0, 2)
    lw["hy_conv_b"] = p["hy_conv_b"][l].reshape(3, 1, BR)
    lw["hy_w1"] = jnp.pad(p["hy_w1"][l], ((0, HY_EMB_PAD - HY_EMB), (0, 0)))
    lw["hy_b1"] = p["hy_b1"][l].reshape(1, HY_HIDDEN)
    lw["hy_w2"] = p["hy_w2"][l]
    lw["hy_b2"] = p["hy_b2"][l].reshape(1, HY_HIDDEN)
    lw["hy_freq"] = p["hy_freq"][l].reshape(1, HY_HIDDEN)
    lw["hy_w3"] = p["hy_w3"][l].reshape(HY_HIDDEN, 2, 2, BR).transpose(2, 0, 1, 3).reshape(2, HY_HIDDEN, 2 * BR)
    deltas = jnp.abs(jnp.linspace(HY_MIN_DECAY, HY_MAX_DECAY, BR, dtype=f32))
    lw["hy_deltas"] = jnp.concatenate([deltas, deltas]).reshape(1, 2 * BR)
    lw["hy_skip"] = p["hy_skip"][l]
    lw["ml_conv_w"] = p["ml_conv_w"][l]
    lw["ml_conv_b"] = p["ml_conv_b"][l].reshape(1, BR)
    lw["ml_wq"] = _block_diag(p["ml_wq"][l]).astype(bf16)
    lw["ml_wk"] = _block_diag(p["ml_wk"][l]).astype(bf16)
    lw["ml_wv"] = _block_diag(p["ml_wv"][l]).astype(bf16)
    bias = jnp.concatenate([p["ml_b_i"][l].reshape(-1), p["ml_b_f"][l].reshape(-1)])
    lw["ml_bias"] = jnp.pad(bias, (LANE_ML_I, 128 - LANE_ML_I - bias.shape[0])).reshape(1, 128)
    lw["ml_ng"] = p["ml_norm_gain"][l].reshape(1, BR)
    lw["ml_skip"] = p["ml_skip"][l].reshape(1, BR)
    return lw


def _mix(proj, small, lw, consts, B, L):
    cosf, sins, emb, tabs = consts
    proj3 = proj.reshape(B, L, N_PROJ)
    small3 = small.reshape(B, L, N_SMALL)
    return (
        _ret_branch(proj3, cosf, sins, B, L).reshape(B * L, BR),
        _gla_branch(proj3, small3, lw["gla_wa"], lw["gla_ba"], lw["gla_ng"], B, L).reshape(B * L, BR),
        _hy_branch(proj, lw, emb, tabs, B, L),
        _ml_branch(proj3, small3, lw["ml_conv_w"], lw["ml_conv_b"], lw["ml_wq"], lw["ml_wk"], lw["ml_wv"],
                   lw["ml_bias"], lw["ml_ng"], lw["ml_skip"], B, L).reshape(B * L, BR),
    )


def kernel(x_prompt, x_sample, c_prompt, c_sample, norm_gain, w_ada, b_ada, w_in, gla_w_a, gla_b_a, gla_norm_gain, hy_conv_w, hy_conv_b, hy_w1, hy_b1, hy_w2, hy_b2, hy_freq, hy_w3, hy_skip, ml_conv_w, ml_conv_b, ml_wq, ml_wk, ml_wv, ml_b_i, ml_b_f, ml_norm_gain, ml_skip, w_branch, w_gate, b_gate, w_out, final_gain):
    p = dict(gla_w_a=gla_w_a, gla_b_a=gla_b_a, gla_norm_gain=gla_norm_gain, hy_conv_w=hy_conv_w,
             hy_conv_b=hy_conv_b, hy_w1=hy_w1, hy_b1=hy_b1, hy_w2=hy_w2, hy_b2=hy_b2, hy_freq=hy_freq,
             hy_w3=hy_w3, hy_skip=hy_skip, ml_conv_w=ml_conv_w, ml_conv_b=ml_conv_b, ml_wq=ml_wq,
             ml_wk=ml_wk, ml_wv=ml_wv, ml_b_i=ml_b_i, ml_b_f=ml_b_f, ml_norm_gain=ml_norm_gain,
             ml_skip=ml_skip)
    groups = []
    for x, c in ((x_prompt, c_prompt), (x_sample, c_sample)):
        B, L, _ = x.shape
        groups.append(dict(B=B, L=L, x=x.reshape(B * L, D_MODEL),
                           consts=_rotary_tables(L) + (_hy_embedding(L), _fft_tables(L))))
    nb = [g["B"] for g in groups]
    rows = -(-sum(nb) // 8) * 8
    c_all = jnp.concatenate([c_prompt, c_sample, jnp.zeros((rows - sum(nb), D_MODEL), f32)], axis=0)
    mod_all = _ada_all(c_all, w_ada, b_ada)

    w_in_p, w_in_s = _permute_w_in(w_in)
    ng = D_MODEL // TN_GATES
    wg = w_gate.astype(bf16).reshape(DEPTH, 4, D_MODEL, ng, TN_GATES).transpose(0, 3, 1, 2, 4)
    wb = w_branch.astype(bf16).reshape(DEPTH, 4, BR, ng, TN_GATES).transpose(0, 3, 1, 2, 4)
    bg = b_gate.reshape(DEPTH, 4, 1, ng, TN_GATES).transpose(0, 3, 1, 2, 4)
    wo = w_out.astype(bf16)
    fg = final_gain.reshape(1, D_MODEL)
    for l in range(DEPTH):
        lw = _layer_weights(l, p)
        gain = norm_gain[l].reshape(1, D_MODEL)
        start = 0
        for g in groups:
            B, L = g["B"], g["L"]
            mod = mod_all[l, start:start + B].reshape(B, 3, D_MODEL)
            start += B
            proj, small, h = _inproj(g["x"], mod, gain, w_in_p[l], w_in_s[l], L)
            branches = _mix(proj, small, lw, g["consts"], B, L)
            merged = _gated_merge(h, branches, wg[l], bg[l], wb[l], L)
            g["x"] = _outproj(merged, wo[l], g["x"], mod, fg, L, l == DEPTH - 1)
    return tuple(g["x"].reshape(g["B"], g["L"], D_MODEL) for g in groups)
```
